```python
import math
import jax
import jax.numpy as jnp
from jax import lax
import numpy as np


D_MODEL = 2048
BATCH = 4
SEQ = 4096
DEPTH = 4

N_MIXERS = 4
PLE_DIM = 256
D_FF = 4 * D_MODEL
EPS = 1e-6
NEG = -1e30
FORCE_BONUS = 1e4

NSA_HEAD_DIM = 128
NSA_HEADS = D_MODEL // NSA_HEAD_DIM
NSA_KV_HEADS = NSA_HEADS // 4
NSA_CMP_STRIDE = 16
NSA_CMP_BLOCK = 2 * NSA_CMP_STRIDE
NSA_SEL_BLOCK = 64
NSA_TOP_N = 16
NSA_WINDOW = 512
NSA_Q_BLOCK = 32
NSA_IN = NSA_HEADS * NSA_HEAD_DIM + 6 * NSA_KV_HEADS * NSA_HEAD_DIM + 3 * NSA_HEADS

M2_D_INNER = 2 * D_MODEL
M2_HEAD_DIM = 64
M2_HEADS = M2_D_INNER // M2_HEAD_DIM
M2_GROUPS = 8
M2_STATE = 128
M2_CONV = 4
M2_CHUNK = 128
M2_CONV_DIM = M2_D_INNER + 2 * M2_GROUPS * M2_STATE
M2_IN = M2_D_INNER + M2_CONV_DIM + M2_HEADS

GDN_HEAD_DIM = 128
GDN_K_HEADS = D_MODEL // GDN_HEAD_DIM
GDN_V_HEADS = 2 * GDN_K_HEADS
GDN_CONV = 4
GDN_CHUNK = 64
GDN_CONV_DIM = (2 * GDN_K_HEADS + GDN_V_HEADS) * GDN_HEAD_DIM
GDN_IN = GDN_CONV_DIM + GDN_V_HEADS * GDN_HEAD_DIM + 2 * GDN_V_HEADS

ML_HEADS = 4
ML_QK_DIM = D_MODEL // (2 * ML_HEADS)
ML_V_DIM = D_MODEL // ML_HEADS
ML_CHUNK = 64
ML_IN = 2 * ML_HEADS * ML_QK_DIM + 2 * ML_HEADS * ML_V_DIM + 2 * ML_HEADS

kernel_name = 'hybrid_nsa_ssd_gdn_mlstm_trunk'


def rms_norm(x, g):
    xf = x.astype(jnp.float32)
    y = xf * lax.rsqrt(jnp.mean(xf * xf, axis=-1, keepdims=True) + EPS)
    return (y * g.astype(jnp.float32)).astype(x.dtype)


def l2_norm(x):
    xf = x.astype(jnp.float32)
    return (xf * lax.rsqrt(jnp.sum(xf * xf, axis=-1, keepdims=True) + EPS)).astype(x.dtype)


def causal_dwconv(x, w):
    k, c = w.shape
    return lax.conv_general_dilated(x, w[:, None, :].astype(x.dtype), window_strides=(1,), padding=[(k - 1, 0)], dimension_numbers=('NWC', 'WIO', 'NWC'), feature_group_count=c)


def masked_softmax(s, mask):
    s = jnp.where(mask, s.astype(jnp.float32), NEG)
    e = jnp.where(mask, jnp.exp(s - jnp.max(s, axis=-1, keepdims=True)), 0.0)
    return e / jnp.maximum(jnp.sum(e, axis=-1, keepdims=True), 1e-30)


def _to_chunks(t, L):
    bsz, seq, h = t.shape[:3]
    t = t.astype(jnp.float32).reshape((bsz, seq // L, L, h) + t.shape[3:])
    return jnp.transpose(t, (1, 0, 3, 2) + tuple(range(4, t.ndim)))


def _from_chunks(t):
    nc, bsz, h, L = t.shape[:4]
    t = jnp.transpose(t, (1, 0, 3, 2) + tuple(range(4, t.ndim)))
    return t.reshape((bsz, nc * L, h) + t.shape[4:])


def nsa_mixer(u, w_in, q_g, k_g, cmp_pos, cmp_w1, cmp_w2, w_out):
    bsz, seq, _ = u.shape
    H, G, dk = NSA_HEADS, NSA_KV_HEADS, NSA_HEAD_DIM
    hpg = H // G
    L, STR, SEL, W, QB = NSA_CMP_BLOCK, NSA_CMP_STRIDE, NSA_SEL_BLOCK, NSA_WINDOW, NSA_Q_BLOCK
    n_cmp = seq // STR - 1
    n_sel = seq // SEL
    n_top = min(NSA_TOP_N, n_sel)
    nq = seq // QB
    scale = dk ** -0.5
    splits = [int(v) for v in np.cumsum([H * dk] + [G * dk] * 6)]
    q, k_c, v_c, k_s, v_s, k_w, v_w, gate = jnp.split(u @ w_in, splits, axis=-1)

    def kv(t):
        return t.reshape(bsz, seq, G, dk)

    q = rms_norm(q.reshape(bsz, seq, H, dk), q_g)
    gate = jax.nn.sigmoid(gate.reshape(bsz, seq, H, 3))

    def compress(t, pos, w1, w2):
        ch = t.reshape(bsz, seq // STR, STR, G, dk)
        blk = jnp.concatenate([ch[:, :-1], ch[:, 1:]], axis=2) + pos[:, None, :]
        return jax.nn.silu(jnp.einsum('bnlgd,lde->bnge', blk, w1)) @ w2

    k_c = rms_norm(compress(kv(k_c), cmp_pos[0], cmp_w1[0], cmp_w2[0]), k_g[0])
    v_c = compress(kv(v_c), cmp_pos[1], cmp_w1[1], cmp_w2[1])
    k_s = rms_norm(kv(k_s), k_g[1]).reshape(bsz, n_sel, SEL, G, dk).transpose(0, 3, 1, 2, 4)
    v_s = kv(v_s).reshape(bsz, n_sel, SEL, G, dk).transpose(0, 3, 1, 2, 4)
    pad = ((0, 0), (W, 0), (0, 0), (0, 0))
    k_w = jnp.pad(rms_norm(kv(k_w), k_g[2]), pad)
    v_w = jnp.pad(kv(v_w), pad)

    c_start = jnp.arange(n_cmp) * STR
    c_end = c_start + L - 1
    s_start = jnp.arange(n_sel) * SEL
    overlap = ((c_start[:, None] < s_start[None, :] + SEL) & (c_end[:, None] >= s_start[None, :])).astype(jnp.float32)
    blk_id = jnp.arange(n_sel)
    take = jax.vmap(jax.vmap(lambda tab, ix: tab[ix]))

    def block(args):
        qb, gb, s0 = args
        t = s0 + jnp.arange(QB)
        sc = jnp.einsum('bghqd,bngd->bghqn', qb, k_c) * scale
        pc = masked_softmax(sc, c_end[None, :] <= t[:, None])
        o_cmp = jnp.einsum('bghqn,bngd->bghqd', pc.astype(v_c.dtype), v_c)
        imp = jnp.einsum('bgqn,nj->bgqj', pc.sum(axis=2), overlap)
        cur = t // SEL
        forced = (blk_id[None, :] == 0) | (blk_id[None, :] == cur[:, None]) | (blk_id[None, :] == cur[:, None] - 1)
        imp = jnp.where(blk_id[None, :] <= cur[:, None], imp + jnp.where(forced, FORCE_BONUS, 0.0), NEG)
        _, idx = lax.top_k(imp, n_top)
        kg = take(k_s, idx).reshape(bsz, G, QB, n_top * SEL, dk)
        vg = take(v_s, idx).reshape(bsz, G, QB, n_top * SEL, dk)
        kpos = (idx[..., None] * SEL + jnp.arange(SEL)).reshape(bsz, G, QB, n_top * SEL)
        ss = jnp.einsum('bghqd,bgqkd->bghqk', qb, kg) * scale
        ps = masked_softmax(ss, (kpos <= t[:, None])[:, :, None])
        o_slc = jnp.einsum('bghqk,bgqkd->bghqd', ps.astype(vg.dtype), vg)
        kwb = lax.dynamic_slice_in_dim(k_w, s0, W + QB, axis=1)
        vwb = lax.dynamic_slice_in_dim(v_w, s0, W + QB, axis=1)
        wpos = s0 - W + jnp.arange(W + QB)
        mw = (wpos[None, :] <= t[:, None]) & (wpos[None, :] > t[:, None] - W) & (wpos[None, :] >= 0)
        sw = jnp.einsum('bghqd,bkgd->bghqk', qb, kwb) * scale
        o_win = jnp.einsum('bghqk,bkgd->bghqd', masked_softmax(sw, mw).astype(vwb.dtype), vwb)
        return gb[..., 0:1] * o_cmp + gb[..., 1:2] * o_slc + gb[..., 2:3] * o_win

    q_blk = q.reshape(bsz, nq, QB, G, hpg, dk).transpose(1, 0, 3, 4, 2, 5)
    g_blk = gate.reshape(bsz, nq, QB, G, hpg, 3).transpose(1, 0, 3, 4, 2, 5)
    o = lax.map(block, (q_blk, g_blk, jnp.arange(nq) * QB))
    o = o.transpose(1, 0, 4, 2, 3, 5).reshape(bsz, seq, H * dk)
    return o @ w_out


def ssd_chunked(x, dt, a, bm, cm):
    bsz, seq, g, r, p = x.shape
    n = bm.shape[-1]
    L = M2_CHUNK

    def chunks(t):
        return t.astype(jnp.float32).reshape((bsz, seq // L, L) + t.shape[2:]).swapaxes(0, 1)

    causal = jnp.tril(jnp.ones((L, L), bool))[None, :, :, None, None]

    def step(state, inp):
        xc, lac, bc, cc = inp
        cum = jnp.cumsum(lac, axis=1)
        decay = jnp.exp(jnp.where(causal, cum[:, :, None] - cum[:, None], -jnp.inf))
        cb = jnp.einsum('btgn,bsgn->btsg', cc, bc)
        y = jnp.einsum('btsg,btsgr,bsgrp->btgrp', cb, decay, xc)
        y = y + jnp.einsum('btgn,bgrpn->btgrp', cc, state) * jnp.exp(cum)[..., None]
        tail = jnp.exp(cum[:, -1:] - cum)
        state = state * jnp.exp(cum[:, -1])[..., None, None] + jnp.einsum('bsgr,bsgrp,bsgn->bgrpn', tail, xc, bc)
        return state, y

    state0 = jnp.zeros((bsz, g, r, p, n), jnp.float32)
    _, y = lax.scan(step, state0, (chunks(x * dt[..., None]), chunks(dt * a), chunks(bm), chunks(cm)))
    return y.swapaxes(0, 1).reshape(bsz, seq, g, r, p)


def mamba2_mixer(u, w_in, conv_w, conv_b, dt_bias, a_log, d_skip, norm_g, w_out):
    bsz, seq, _ = u.shape
    Di, H, P, G, N = M2_D_INNER, M2_HEADS, M2_HEAD_DIM, M2_GROUPS, M2_STATE
    R = H // G
    z, xbc, dt = jnp.split(u @ w_in, [Di, Di + M2_CONV_DIM], axis=-1)
    xbc = jax.nn.silu(causal_dwconv(xbc, conv_w) + conv_b)
    xs, bm, cm = jnp.split(xbc, [Di, Di + G * N], axis=-1)
    dt = jax.nn.softplus((dt + dt_bias).astype(jnp.float32)).reshape(bsz, seq, G, R)
    a = -jnp.exp(a_log.astype(jnp.float32)).reshape(G, R)
    xs = xs.reshape(bsz, seq, G, R, P)
    y = ssd_chunked(xs, dt, a, bm.reshape(bsz, seq, G, N), cm.reshape(bsz, seq, G, N))
    y = y + d_skip.reshape(G, R, 1) * xs
    y = (y * jax.nn.silu(z).reshape(bsz, seq, G, R, P)).reshape(bsz, seq, G, Di // G)
    y = rms_norm(y, norm_g.reshape(G, Di // G)).reshape(bsz, seq, Di)
    return y @ w_out


def gated_delta_chunked(q, k, v, beta, g):
    L = GDN_CHUNK
    bsz, seq, h, dk = k.shape
    dv = v.shape[-1]
    incl = jnp.tril(jnp.ones((L, L), bool))
    strict = jnp.tril(jnp.ones((L, L), bool), -1)
    eye = jnp.eye(L, dtype=jnp.float32)

    def step(state, inp):
        qc, kc, vc, bc, gc = inp
        cum = jnp.cumsum(gc, axis=-1)
        dmask = jnp.exp(jnp.where(incl, cum[..., :, None] - cum[..., None, :], -jnp.inf))
        kb = kc * bc[..., None]
        tri = eye + jnp.where(strict, jnp.einsum('bhid,bhjd->bhij', kb, kc) * dmask, 0.0)
        rhs = jnp.concatenate([vc * bc[..., None], kb * jnp.exp(cum)[..., None]], axis=-1)
        sol = lax.linalg.triangular_solve(tri, rhs, left_side=True, lower=True, unit_diagonal=True)
        v_new = sol[..., :dv] - jnp.einsum('bhik,bhkv->bhiv', sol[..., dv:], state)
        attn = jnp.einsum('bhid,bhjd->bhij', qc, kc) * dmask
        o = jnp.einsum('bhik,bhkv->bhiv', qc * jnp.exp(cum)[..., None], state) + jnp.einsum('bhij,bhjv->bhiv', attn, v_new)
        state = state * jnp.exp(cum[..., -1])[..., None, None] + jnp.einsum('bhik,bhiv->bhkv', kc * jnp.exp(cum[..., -1:] - cum)[..., None], v_new)
        return state, o

    state0 = jnp.zeros((bsz, h, dk, dv), jnp.float32)
    _, o = lax.scan(step, state0, (_to_chunks(q, L), _to_chunks(k, L), _to_chunks(v, L), _to_chunks(beta, L), _to_chunks(g, L)))
    return _from_chunks(o)


def gdn_mixer(u, w_in, conv_w, a_log, dt_bias, norm_g, w_out):
    bsz, seq, _ = u.shape
    Hk, Hv, dh = GDN_K_HEADS, GDN_V_HEADS, GDN_HEAD_DIM
    qkv, z, b, a = jnp.split(u @ w_in, [GDN_CONV_DIM, GDN_CONV_DIM + Hv * dh, GDN_CONV_DIM + Hv * dh + Hv], axis=-1)
    qkv = jax.nn.silu(causal_dwconv(qkv, conv_w))
    q, k, v = jnp.split(qkv, [Hk * dh, 2 * Hk * dh], axis=-1)
    rep = Hv // Hk
    q = jnp.repeat(l2_norm(q.reshape(bsz, seq, Hk, dh)) * dh ** -0.5, rep, axis=2)
    k = jnp.repeat(l2_norm(k.reshape(bsz, seq, Hk, dh)), rep, axis=2)
    beta = jax.nn.sigmoid(b.astype(jnp.float32))
    g = -jnp.exp(a_log.astype(jnp.float32)) * jax.nn.softplus((a + dt_bias).astype(jnp.float32))
    o = gated_delta_chunked(q, k, v.reshape(bsz, seq, Hv, dh), beta, g)
    o = rms_norm(o, norm_g) * jax.nn.silu(z.reshape(bsz, seq, Hv, dh))
    return o.reshape(bsz, seq, Hv * dh) @ w_out


def mlstm_chunked(q, k, v, li, lf):
    L = ML_CHUNK
    bsz, seq, h, dq = q.shape
    dv = v.shape[-1]
    incl = jnp.tril(jnp.ones((L, L), bool))

    def step(carry, inp):
        c, n, m = carry
        qc, kc, vc, lic, lfc = inp
        b = jnp.cumsum(lfc, axis=-1)
        dlog = jnp.where(incl, b[..., :, None] - b[..., None, :] + lic[..., None, :], -jnp.inf)
        inter = b + m[..., None]
        m_t = jnp.maximum(inter, jnp.max(dlog, axis=-1))
        w_inter = jnp.exp(inter - m_t)
        s = jnp.einsum('bhtd,bhsd->bhts', qc, kc) * jnp.exp(dlog - m_t[..., None])
        num = jnp.einsum('bhts,bhsv->bhtv', s, vc) + w_inter[..., None] * jnp.einsum('bhtd,bhdv->bhtv', qc, c)
        den = jnp.sum(s, axis=-1) + w_inter * jnp.einsum('bhtd,bhd->bht', qc, n)
        hc = num / jnp.maximum(jnp.abs(den), jnp.exp(-m_t))[..., None]
        tot = b[..., -1]
        ulog = tot[..., None] - b + lic
        m_new = jnp.maximum(tot + m, jnp.max(ulog, axis=-1))
        wu = jnp.exp(ulog - m_new[..., None])
        dec = jnp.exp(tot + m - m_new)
        c = dec[..., None, None] * c + jnp.einsum('bhs,bhsd,bhsv->bhdv', wu, kc, vc)
        n = dec[..., None] * n + jnp.einsum('bhs,bhsd->bhd', wu, kc)
        return (c, n, m_new), hc

    carry0 = (jnp.zeros((bsz, h, dq, dv), jnp.float32), jnp.zeros((bsz, h, dq), jnp.float32), jnp.zeros((bsz, h), jnp.float32))
    _, hs = lax.scan(step, carry0, (_to_chunks(q, L), _to_chunks(k, L), _to_chunks(v, L), _to_chunks(li, L), _to_chunks(lf, L)))
    return _from_chunks(hs)


def mlstm_mixer(u, w_in, gate_b, norm_g, w_out):
    bsz, seq, _ = u.shape
    H, dq, dv = ML_HEADS, ML_QK_DIM, ML_V_DIM
    q, k, v, o, gi = jnp.split(u @ w_in, [H * dq, 2 * H * dq, 2 * H * dq + H * dv, 2 * H * dq + 2 * H * dv], axis=-1)
    gi = (gi + gate_b).astype(jnp.float32)
    hh = mlstm_chunked(q.reshape(bsz, seq, H, dq) * dq ** -0.5, k.reshape(bsz, seq, H, dq), v.reshape(bsz, seq, H, dv), gi[..., :H], jax.nn.log_sigmoid(gi[..., H:]))
    hh = rms_norm(hh, norm_g.reshape(H, dv)).reshape(bsz, seq, H * dv)
    return (jax.nn.sigmoid(o) * hh) @ w_out


def setup_inputs(seed: int = 0) -> dict:
    key = jax.random.key(seed)
    keys = iter(jax.random.split(key, 48))
    f32 = jnp.float32
    D = D_MODEL
    res = (2 * DEPTH) ** -0.5

    def normal(shape, std):
        return jax.random.normal(next(keys), shape, f32) * std

    def gain(shape):
        return 1.0 + 0.02 * jax.random.normal(next(keys), shape, f32)

    def unif(shape, lo, hi):
        return jax.random.uniform(next(keys), shape, f32, lo, hi)

    def dt_bias(shape):
        dt = jnp.exp(unif(shape, math.log(1e-3), math.log(1e-1)))
        return dt + jnp.log(-jnp.expm1(-dt))

    n_a, n_b, n_c, n_d = [len(range(m, DEPTH, N_MIXERS)) for m in range(N_MIXERS)]
    dk = NSA_HEAD_DIM
    return {
        'x': normal((BATCH, SEQ, D), 1.0),
        'p': normal((DEPTH, BATCH, SEQ, PLE_DIM), 1.0),
        'norm_mix_g': gain((DEPTH, D)),
        'norm_mlp_g': gain((DEPTH, D)),
        'mlp_w1': normal((DEPTH, D, D_FF), D ** -0.5),
        'mlp_w2': normal((DEPTH, D_FF, D), res * D_FF ** -0.5),
        'ple_norm_g': gain((DEPTH, D)),
        'ple_w_gate': normal((DEPTH, D, D), D ** -0.5),
        'ple_w_proj': normal((DEPTH, PLE_DIM, D), res * PLE_DIM ** -0.5),
        'nsa_w_in': normal((n_a, D, NSA_IN), D ** -0.5),
        'nsa_q_g': gain((n_a, dk)),
        'nsa_k_g': gain((n_a, 3, dk)),
        'nsa_cmp_pos': normal((n_a, 2, NSA_CMP_BLOCK, dk), 0.02),
        'nsa_cmp_w1': normal((n_a, 2, NSA_CMP_BLOCK, dk, dk), (NSA_CMP_BLOCK * dk) ** -0.5),
        'nsa_cmp_w2': normal((n_a, 2, dk, dk), dk ** -0.5),
        'nsa_w_out': normal((n_a, NSA_HEADS * dk, D), res * (NSA_HEADS * dk) ** -0.5),
        'm2_w_in': normal((n_b, D, M2_IN), D ** -0.5),
        'm2_conv_w': normal((n_b, M2_CONV, M2_CONV_DIM), M2_CONV ** -0.5),
        'm2_conv_b': normal((n_b, M2_CONV_DIM), 0.02),
        'm2_dt_bias': dt_bias((n_b, M2_HEADS)),
        'm2_a_log': jnp.log(unif((n_b, M2_HEADS), 1.0, 16.0)),
        'm2_d': gain((n_b, M2_HEADS)),
        'm2_norm_g': gain((n_b, M2_D_INNER)),
        'm2_w_out': normal((n_b, M2_D_INNER, D), res * M2_D_INNER ** -0.5),
        'gdn_w_in': normal((n_c, D, GDN_IN), D ** -0.5),
        'gdn_conv_w': normal((n_c, GDN_CONV, GDN_CONV_DIM), GDN_CONV ** -0.5),
        'gdn_a_log': jnp.log(unif((n_c, GDN_V_HEADS), 1.0, 16.0)),
        'gdn_dt_bias': dt_bias((n_c, GDN_V_HEADS)),
        'gdn_norm_g': gain((n_c, GDN_HEAD_DIM)),
        'gdn_w_out': normal((n_c, GDN_V_HEADS * GDN_HEAD_DIM, D), res * (GDN_V_HEADS * GDN_HEAD_DIM) ** -0.5),
        'ml_w_in': normal((n_d, D, ML_IN), D ** -0.5),
        'ml_gate_b': jnp.concatenate([normal((n_d, ML_HEADS), 0.1), unif((n_d, ML_HEADS), 3.0, 6.0)], axis=-1),
        'ml_norm_g': gain((n_d, ML_HEADS * ML_V_DIM)),
        'ml_w_out': normal((n_d, ML_HEADS * ML_V_DIM, D), res * (ML_HEADS * ML_V_DIM) ** -0.5),
    }


def reference(x, p, norm_mix_g, norm_mlp_g, mlp_w1, mlp_w2, ple_norm_g, ple_w_gate, ple_w_proj, nsa_w_in, nsa_q_g, nsa_k_g, nsa_cmp_pos, nsa_cmp_w1, nsa_cmp_w2, nsa_w_out, m2_w_in, m2_conv_w, m2_conv_b, m2_dt_bias, m2_a_log, m2_d, m2_norm_g, m2_w_out, gdn_w_in, gdn_conv_w, gdn_a_log, gdn_dt_bias, gdn_norm_g, gdn_w_out, ml_w_in, ml_gate_b, ml_norm_g, ml_w_out):
    h = x
    for i in range(DEPTH):
        kind, j = i % N_MIXERS, i // N_MIXERS
        u = rms_norm(h, norm_mix_g[i])
        if kind == 0:
            mix = nsa_mixer(u, nsa_w_in[j], nsa_q_g[j], nsa_k_g[j], nsa_cmp_pos[j], nsa_cmp_w1[j], nsa_cmp_w2[j], nsa_w_out[j])
        elif kind == 1:
            mix = mamba2_mixer(u, m2_w_in[j], m2_conv_w[j], m2_conv_b[j], m2_dt_bias[j], m2_a_log[j], m2_d[j], m2_norm_g[j], m2_w_out[j])
        elif kind == 2:
            mix = gdn_mixer(u, gdn_w_in[j], gdn_conv_w[j], gdn_a_log[j], gdn_dt_bias[j], gdn_norm_g[j], gdn_w_out[j])
        else:
            mix = mlstm_mixer(u, ml_w_in[j], ml_gate_b[j], ml_norm_g[j], ml_w_out[j])
        h = h + mix
        u = rms_norm(h, norm_mlp_g[i])
        h = h + jnp.square(jax.nn.relu(u @ mlp_w1[i])) @ mlp_w2[i]
        gate = jax.nn.sigmoid(rms_norm(h, ple_norm_g[i]) @ ple_w_gate[i])
        h = h + gate * (p[i] @ ple_w_proj[i])
    return h
```

```python
import functools
import math

import jax
import jax.numpy as jnp
from jax import lax
from jax.experimental import pallas as pl
from jax.experimental.pallas import tpu as pltpu

F32 = jnp.float32
BF16 = jnp.bfloat16
EPS = 1e-6
NEG = -1e30
HIGHEST = lax.Precision.HIGHEST

VMEM_LIMIT_BYTES = 56 * 1024 * 1024
LANES = 128


def _cparams(sem):
    return pltpu.CompilerParams(dimension_semantics=sem, vmem_limit_bytes=VMEM_LIMIT_BYTES)


def _silu(x):
    return x * (1.0 / (1.0 + jnp.exp(-x)))


def _sigmoid(x):
    return 1.0 / (1.0 + jnp.exp(-x))


def _softplus(x):
    return jnp.maximum(x, 0.0) + jnp.log(1.0 + jnp.exp(-jnp.abs(x)))


def _dot(a, b, **kw):
    return jnp.dot(a, b, preferred_element_type=F32, **kw)


def _dot_nt(a, b, **kw):
    return lax.dot_general(a, b, (((1,), (1,)), ((), ())), preferred_element_type=F32, **kw)


def _dot_tn(a, b, **kw):
    return lax.dot_general(a, b, (((0,), (0,)), ((), ())), preferred_element_type=F32, **kw)


def _norm_mm_kernel(x_ref, g_ref, w_ref, o_ref, xn_ref, *, act):
    @pl.when(pl.program_id(1) == 0)
    def _():
        x = x_ref[...]
        ms = jnp.mean(x * x, axis=-1, keepdims=True)
        xn_ref[...] = (x * lax.rsqrt(ms + EPS) * g_ref[...]).astype(BF16)

    acc = _dot(xn_ref[...], w_ref[...])
    if act == "relu2":
        r = jnp.maximum(acc, 0.0)
        acc = r * r
    o_ref[...] = acc.astype(o_ref.dtype)


def norm_matmul(x, g, w, *, act=None, out_dtype=F32, tm=512, tn=512):
    m, k = x.shape
    n = w.shape[1]
    tm, tn = min(tm, m), min(tn, n)
    assert m % tm == 0 and n % tn == 0
    return pl.pallas_call(
        functools.partial(_norm_mm_kernel, act=act),
        grid=(m // tm, n // tn),
        in_specs=[
            pl.BlockSpec((tm, k), lambda i, j: (i, 0)),
            pl.BlockSpec((1, k), lambda i, j: (0, 0)),
            pl.BlockSpec((k, tn), lambda i, j: (0, j)),
        ],
        out_specs=pl.BlockSpec((tm, tn), lambda i, j: (i, j)),
        out_shape=jax.ShapeDtypeStruct((m, n), out_dtype),
        scratch_shapes=[pltpu.VMEM((tm, k), BF16)],
        compiler_params=_cparams(("parallel", "arbitrary")),
        name="norm_matmul",
    )(x, g.reshape(1, k), w)


def _mm_res_kernel(x_ref, w_ref, r_ref, o_ref, acc_ref):
    kk = pl.program_id(2)

    @pl.when(kk == 0)
    def _():
        acc_ref[...] = r_ref[...]

    acc_ref[...] += _dot(x_ref[...], w_ref[...])

    @pl.when(kk == pl.num_programs(2) - 1)
    def _():
        o_ref[...] = acc_ref[...]


def matmul_residual(x, w, res, *, tm=512, tn=512, tk=2048):
    m, k = x.shape
    n = w.shape[1]
    tm, tn, tk = min(tm, m), min(tn, n), min(tk, k)
    assert m % tm == 0 and n % tn == 0 and k % tk == 0
    return pl.pallas_call(
        _mm_res_kernel,
        grid=(m // tm, n // tn, k // tk),
        in_specs=[
            pl.BlockSpec((tm, tk), lambda i, j, l: (i, l)),
            pl.BlockSpec((tk, tn), lambda i, j, l: (l, j)),
            pl.BlockSpec((tm, tn), lambda i, j, l: (i, j)),
        ],
        out_specs=pl.BlockSpec((tm, tn), lambda i, j, l: (i, j)),
        out_shape=jax.ShapeDtypeStruct((m, n), F32),
        scratch_shapes=[pltpu.VMEM((tm, tn), F32)],
        compiler_params=_cparams(("parallel", "parallel", "arbitrary")),
        name="matmul_residual",
    )(x, w, res)


def _ple_kernel(x_ref, g_ref, wg_ref, p_ref, wp_ref, r_ref, o_ref, xn_ref):
    @pl.when(pl.program_id(1) == 0)
    def _():
        x = x_ref[...]
        ms = jnp.mean(x * x, axis=-1, keepdims=True)
        xn_ref[...] = (x * lax.rsqrt(ms + EPS) * g_ref[...]).astype(BF16)

    gate = _sigmoid(_dot(xn_ref[...], wg_ref[...]))
    proj = _dot(p_ref[...].astype(BF16), wp_ref[...])
    o_ref[...] = r_ref[...] + gate * proj


def ple_update(h, g, wg, p, wp, *, tm=512, tn=512):
    m, k = h.shape
    n = wg.shape[1]
    kp = p.shape[1]
    tm, tn = min(tm, m), min(tn, n)
    assert m % tm == 0 and n % tn == 0
    return pl.pallas_call(
        _ple_kernel,
        grid=(m // tm, n // tn),
        in_specs=[
            pl.BlockSpec((tm, k), lambda i, j: (i, 0)),
            pl.BlockSpec((1, k), lambda i, j: (0, 0)),
            pl.BlockSpec((k, tn), lambda i, j: (0, j)),
            pl.BlockSpec((tm, kp), lambda i, j: (i, 0)),
            pl.BlockSpec((kp, tn), lambda i, j: (0, j)),
            pl.BlockSpec((tm, tn), lambda i, j: (i, j)),
        ],
        out_specs=pl.BlockSpec((tm, tn), lambda i, j: (i, j)),
        out_shape=jax.ShapeDtypeStruct((m, n), F32),
        scratch_shapes=[pltpu.VMEM((tm, k), BF16)],
        compiler_params=_cparams(("parallel", "arbitrary")),
        name="ple_update",
    )(h, g.reshape(1, k), wg, p, wp, h)


def mlp_block(h, g, w1, w2):
    a = norm_matmul(h, g, w1, act="relu2", out_dtype=BF16)
    return matmul_residual(a, w2, h)


def _pad_cols(w, n):
    return jnp.pad(w, ((0, 0), (0, n - w.shape[1])))


def _iota2(shape, dim):
    return lax.broadcasted_iota(jnp.int32, shape, dim)


def _col_to_row(col, eye):
    return jnp.sum(jnp.where(eye, col, 0.0), axis=0, keepdims=True)


def _log_sigmoid(x):
    return jnp.minimum(x, 0.0) - jnp.log(1.0 + jnp.exp(-jnp.abs(x)))


ML_HEADS = 4
ML_CHUNK = 64


def _mlstm_kernel(q_ref, k_ref, v_ref, og_ref, gt_ref, gb_ref, ng_ref, o_ref, c_ref, n_ref, m_ref, *, dq, dv):
    L = ML_CHUNK

    @pl.when(pl.program_id(1) == 0)
    def _():
        c_ref[...] = jnp.zeros_like(c_ref)
        n_ref[...] = jnp.zeros_like(n_ref)
        m_ref[...] = jnp.zeros_like(m_ref)

    row = _iota2((L, L), 0)
    col = _iota2((L, L), 1)
    incl = row >= col
    eye = row == col
    gates = gt_ref[...] + gb_ref[...]
    scale = dq ** -0.5
    for h in range(ML_HEADS):
        q = q_ref[:, h * dq:(h + 1) * dq] * scale
        k = k_ref[:, h * dq:(h + 1) * dq]
        v = v_ref[:, h * dv:(h + 1) * dv]
        li_c = gates[:, h:h + 1]
        lf_c = _log_sigmoid(gates[:, ML_HEADS + h:ML_HEADS + h + 1])
        li_r = _col_to_row(li_c, eye)
        lf_r = _col_to_row(lf_c, eye)
        b_c = jnp.sum(jnp.where(incl, lf_r, 0.0), axis=1, keepdims=True)
        b_r = _col_to_row(b_c, eye)
        m_prev = m_ref[h]
        dlog = jnp.where(incl, b_c - b_r + li_r, NEG)
        inter = b_c + m_prev
        m_t = jnp.maximum(inter, jnp.max(dlog, axis=1, keepdims=True))
        w_inter = jnp.exp(inter - m_t)
        qb = q.astype(BF16)
        kb = k.astype(BF16)
        vb = v.astype(BF16)
        s = _dot_nt(qb, kb) * jnp.where(incl, jnp.exp(dlog - m_t), 0.0)
        c_prev = c_ref[h]
        num = _dot(s.astype(BF16), vb) + w_inter * _dot(qb, c_prev.astype(BF16))
        qn = jnp.sum(q * n_ref[h], axis=1, keepdims=True)
        den = jnp.sum(s, axis=1, keepdims=True) + w_inter * qn
        hc = num / jnp.maximum(jnp.abs(den), jnp.exp(-m_t))
        tot = b_c[L - 1:L, :]
        ulog_c = tot - b_c + li_c
        m_new = jnp.maximum(tot + m_prev, jnp.max(ulog_c, axis=0, keepdims=True))
        wu_c = jnp.exp(ulog_c - m_new)
        dec = jnp.exp(tot + m_prev - m_new)
        kw = k * wu_c
        c_ref[h] = dec * c_prev + _dot_tn(kw.astype(BF16), vb)
        n_ref[h] = dec * n_ref[h] + jnp.sum(kw, axis=0, keepdims=True)
        m_ref[h] = m_new
        ms = jnp.mean(hc * hc, axis=1, keepdims=True)
        hn = hc * lax.rsqrt(ms + EPS) * ng_ref[:, h * dv:(h + 1) * dv]
        o_ref[:, h * dv:(h + 1) * dv] = (_sigmoid(og_ref[:, h * dv:(h + 1) * dv]) * hn).astype(o_ref.dtype)


def mlstm_core(proj, gates, gate_b, norm_g, bsz, seq):
    H, L = ML_HEADS, ML_CHUNK
    t = proj.shape[0]
    dq = proj.shape[1] // (6 * H)
    dv = 2 * dq
    nc = seq // L
    gb = jnp.pad(gate_b, (0, LANES - gate_b.shape[0])).reshape(1, LANES)
    rowmap = lambda b, c: b * nc + c
    return pl.pallas_call(
        functools.partial(_mlstm_kernel, dq=dq, dv=dv),
        grid=(bsz, nc),
        in_specs=[
            pl.BlockSpec((L, H * dq), lambda b, c: (rowmap(b, c), 0)),
            pl.BlockSpec((L, H * dq), lambda b, c: (rowmap(b, c), 1)),
            pl.BlockSpec((L, H * dv), lambda b, c: (rowmap(b, c), 1)),
            pl.BlockSpec((L, H * dv), lambda b, c: (rowmap(b, c), 2)),
            pl.BlockSpec((L, LANES), lambda b, c: (rowmap(b, c), 0)),
            pl.BlockSpec((1, LANES), lambda b, c: (0, 0)),
            pl.BlockSpec((1, H * dv), lambda b, c: (0, 0)),
        ],
        out_specs=pl.BlockSpec((L, H * dv), lambda b, c: (rowmap(b, c), 0)),
        out_shape=jax.ShapeDtypeStruct((t, H * dv), BF16),
        scratch_shapes=[
            pltpu.VMEM((H, dq, dv), F32),
            pltpu.VMEM((H, 1, dq), F32),
            pltpu.VMEM((H, 1, 1), F32),
        ],
        compiler_params=_cparams(("parallel", "arbitrary")),
        name="mlstm_core",
    )(proj, proj, proj, proj, gates, gb, norm_g.reshape(1, H * dv))


def mlstm_layer(h, g_mix, w_in, gate_b, norm_g, w_out, bsz, seq):
    nmain = w_in.shape[1] - 2 * ML_HEADS
    proj = norm_matmul(h, g_mix, w_in[:, :nmain].astype(BF16))
    gates = norm_matmul(h, g_mix, _pad_cols(w_in[:, nmain:], LANES).astype(BF16))
    y = mlstm_core(proj, gates, gate_b, norm_g, bsz, seq)
    return matmul_residual(y, w_out.astype(BF16), h)


M2_CHUNK = 128
M2_GROUPS = 8
M2_STATE = 128
M2_HEAD_DIM = 64
CONV_K = 4
CARRY_ROWS = 8


def _causal_conv_silu(x_raw, carry_ref, w, b):
    n = x_raw.shape[0]
    xe = jnp.concatenate([carry_ref[...], x_raw], axis=0)
    acc = x_raw * w[CONV_K - 1:CONV_K, :]
    if b is not None:
        acc = acc + b
    for k in range(CONV_K - 1):
        sh = CONV_K - 1 - k
        acc = acc + pltpu.roll(xe, sh, axis=0)[CARRY_ROWS:, :] * w[k:k + 1, :]
    carry_ref[...] = x_raw[n - CARRY_ROWS:, :]
    return _silu(acc)


def _ssd_kernel(z_ref, xs_ref, bm_ref, cm_ref, dt_ref, cwx_ref, cwb_ref, cwc_ref, cbx_ref, cbb_ref, cbc_ref,
                dtb_ref, alog_ref, dsk_ref, ng_ref, o_ref, st_ref, cx_ref, cbm_ref, ccm_ref, xs_s, y_s):
    L, P = M2_CHUNK, M2_HEAD_DIM
    R = st_ref.shape[0]

    @pl.when(pl.program_id(2) == 0)
    def _():
        st_ref[...] = jnp.zeros_like(st_ref)
        cx_ref[...] = jnp.zeros_like(cx_ref)
        cbm_ref[...] = jnp.zeros_like(cbm_ref)
        ccm_ref[...] = jnp.zeros_like(ccm_ref)

    xs = _causal_conv_silu(xs_ref[...], cx_ref, cwx_ref[...], cbx_ref[...])
    bm = _causal_conv_silu(bm_ref[...], cbm_ref, cwb_ref[...], cbb_ref[...])
    cm = _causal_conv_silu(cm_ref[...], ccm_ref, cwc_ref[...], cbc_ref[...])
    xs_s[...] = xs
    dt = _softplus(dt_ref[0, 0] + dtb_ref[0])
    lac = dt * (-jnp.exp(alog_ref[0]))
    row = _iota2((L, L), 0)
    col = _iota2((L, L), 1)
    tri = row >= col
    eye = row == col
    cb = _dot_nt(cm.astype(BF16), bm.astype(BF16))
    for r in range(R):
        dt_c = dt[:, r:r + 1]
        dt_r = _col_to_row(dt_c, eye)
        lac_r = _col_to_row(lac[:, r:r + 1], eye)
        cum_c = jnp.sum(jnp.where(tri, lac_r, 0.0), axis=1, keepdims=True)
        cum_r = _col_to_row(cum_c, eye)
        dec = jnp.exp(jnp.where(tri, cum_c - cum_r, NEG))
        mm = cb * dec * dt_r
        ce = cm * jnp.exp(cum_c)
        lhs = jnp.concatenate([mm, ce], axis=1).astype(BF16)
        xs_r = xs_s[:, r * P:(r + 1) * P]
        st = st_ref[r]
        rhs = jnp.concatenate([xs_r, st], axis=0).astype(BF16)
        y_s[:, r * P:(r + 1) * P] = _dot(lhs, rhs)
        cl = cum_c[L - 1:L, :]
        wgt = dt_c * jnp.exp(cl - cum_c)
        st_ref[r] = st * jnp.exp(cl) + _dot_tn((bm * wgt).astype(BF16), xs_r.astype(BF16))
    y = y_s[...] + dsk_ref[...] * xs
    y = y * _silu(z_ref[...])
    ms = jnp.mean(y * y, axis=1, keepdims=True)
    o_ref[...] = (y * lax.rsqrt(ms + EPS) * ng_ref[...]).astype(o_ref.dtype)


def ssd_core(proj, dt_raw, conv_w, conv_b, dt_bias, a_log, d_skip, norm_g, bsz, seq):
    L, G, N, P = M2_CHUNK, M2_GROUPS, M2_STATE, M2_HEAD_DIM
    t = proj.shape[0]
    heads = dt_raw.shape[1]
    R = heads // G
    di = heads * P
    gw = di // G
    nc = seq // L
    dtc = dt_raw.reshape(bsz, seq, G, R).transpose(0, 2, 1, 3)
    cw = conv_w
    cbias = conv_b.reshape(1, -1)
    row = lambda b, g, c: b * nc + c
    nzx = di // gw
    nb0 = 2 * di // N
    cwb0 = di // N
    return pl.pallas_call(
        _ssd_kernel,
        grid=(bsz, G, nc),
        in_specs=[
            pl.BlockSpec((L, gw), lambda b, g, c: (row(b, g, c), g)),
            pl.BlockSpec((L, gw), lambda b, g, c: (row(b, g, c), nzx + g)),
            pl.BlockSpec((L, N), lambda b, g, c: (row(b, g, c), nb0 + g)),
            pl.BlockSpec((L, N), lambda b, g, c: (row(b, g, c), nb0 + G + g)),
            pl.BlockSpec((1, 1, L, R), lambda b, g, c: (b, g, c, 0)),
            pl.BlockSpec((CONV_K, gw), lambda b, g, c: (0, g)),
            pl.BlockSpec((CONV_K, N), lambda b, g, c: (0, cwb0 + g)),
            pl.BlockSpec((CONV_K, N), lambda b, g, c: (0, cwb0 + G + g)),
            pl.BlockSpec((1, gw), lambda b, g, c: (0, g)),
            pl.BlockSpec((1, N), lambda b, g, c: (0, cwb0 + g)),
            pl.BlockSpec((1, N), lambda b, g, c: (0, cwb0 + G + g)),
            pl.BlockSpec((1, 1, R), lambda b, g, c: (g, 0, 0)),
            pl.BlockSpec((1, 1, R), lambda b, g, c: (g, 0, 0)),
            pl.BlockSpec((1, gw), lambda b, g, c: (0, g)),
            pl.BlockSpec((1, gw), lambda b, g, c: (0, g)),
        ],
        out_specs=pl.BlockSpec((L, gw), lambda b, g, c: (row(b, g, c), g)),
        out_shape=jax.ShapeDtypeStruct((t, di), BF16),
        scratch_shapes=[
            pltpu.VMEM((R, N, P), F32),
            pltpu.VMEM((CARRY_ROWS, gw), F32),
            pltpu.VMEM((CARRY_ROWS, N), F32),
            pltpu.VMEM((CARRY_ROWS, N), F32),
            pltpu.VMEM((L, gw), F32),
            pltpu.VMEM((L, gw), F32),
        ],
        compiler_params=_cparams(("parallel", "parallel", "arbitrary")),
        name="ssd_core",
    )(proj, proj, proj, proj, dtc, cw, cw, cw, cbias, cbias, cbias,
      dt_bias.reshape(G, 1, R), a_log.reshape(G, 1, R),
      jnp.repeat(d_skip, P).reshape(1, di), norm_g.reshape(1, di))


def mamba2_layer(h, g_mix, w_in, conv_w, conv_b, dt_bias, a_log, d_skip, norm_g, w_out, bsz, seq):
    heads = dt_bias.shape[0]
    nmain = w_in.shape[1] - heads
    proj = norm_matmul(h, g_mix, w_in[:, :nmain].astype(BF16))
    dt_raw = norm_matmul(h, g_mix, _pad_cols(w_in[:, nmain:], LANES).astype(BF16))[:, :heads]
    y = ssd_core(proj, dt_raw, conv_w, conv_b, dt_bias, a_log, d_skip, norm_g, bsz, seq)
    return matmul_residual(y, w_out.astype(BF16), h)


GDN_CHUNK = 64
GDN_DH = 128
GDN_STACK = 4
GDN_STACKS = 2


def _split_bf16(a):
    hi = a.astype(BF16)
    lo = (a - hi.astype(F32)).astype(BF16)
    return hi, lo


def _dot3(a, b):
    ah, al = _split_bf16(a)
    bh, bl = _split_bf16(b)
    return _dot(ah, bh) + _dot(ah, bl) + _dot(al, bh)


def _inv_unit_lower(nmat, eye_f):
    a = eye_f - nmat
    pw = nmat
    for _ in range(5):
        pw = _dot3(pw, pw)
        a = a + _dot3(a, pw)
    return a


def _l2norm(x):
    return x * lax.rsqrt(jnp.sum(x * x, axis=1, keepdims=True) + EPS)


def _gdn_kernel(q_ref, k_ref, v_ref, z_ref, ba_ref, cwq_ref, cwk_ref, cwv_ref, alog_ref, dtb_ref, ng_ref,
                o_ref, st_ref, cq_ref, ck_ref, cv_ref):
    L, DH, HS = GDN_CHUNK, GDN_DH, GDN_STACK
    nv = GDN_STACK * GDN_STACKS
    SL = HS * L

    @pl.when(pl.program_id(2) == 0)
    def _():
        st_ref[...] = jnp.zeros_like(st_ref)
        cq_ref[...] = jnp.zeros_like(cq_ref)
        ck_ref[...] = jnp.zeros_like(ck_ref)
        cv_ref[...] = jnp.zeros_like(cv_ref)

    q = _causal_conv_silu(q_ref[...], cq_ref, cwq_ref[...], None)
    k = _causal_conv_silu(k_ref[...], ck_ref, cwk_ref[...], None)
    v = _causal_conv_silu(v_ref[...], cv_ref, cwv_ref[...], None)
    ba = ba_ref[0, 0]
    beta = _sigmoid(ba[:, :nv])
    gg = -jnp.exp(alog_ref[0]) * _softplus(ba[:, nv:2 * nv] + dtb_ref[0])
    qn = [_l2norm(q[:, i * DH:(i + 1) * DH]) * DH ** -0.5 for i in range(nv // 2)]
    kn = [_l2norm(k[:, i * DH:(i + 1) * DH]) for i in range(nv // 2)]

    ri = _iota2((SL, SL), 0)
    ci = _iota2((SL, SL), 1)
    same = (ri // L) == (ci // L)
    incl = same & (ri >= ci)
    strict = same & (ri > ci)
    eye = ri == ci
    eye_f = eye.astype(F32)
    last = same & ((ci % L) == L - 1)

    for s in range(GDN_STACKS):
        hv0 = s * HS
        kst = jnp.concatenate([kn[(hv0 + i) // 2] for i in range(HS)], axis=0)
        qst = jnp.concatenate([qn[(hv0 + i) // 2] for i in range(HS)], axis=0)
        vst = jnp.concatenate([v[:, (hv0 + i) * DH:(hv0 + i + 1) * DH] for i in range(HS)], axis=0)
        beta_c = jnp.concatenate([beta[:, hv0 + i:hv0 + i + 1] for i in range(HS)], axis=0)
        g_c = jnp.concatenate([gg[:, hv0 + i:hv0 + i + 1] for i in range(HS)], axis=0)
        g_r = _col_to_row(g_c, eye)
        cum_c = jnp.sum(jnp.where(incl, g_r, 0.0), axis=1, keepdims=True)
        cum_r = _col_to_row(cum_c, eye)
        dmask = jnp.exp(jnp.where(incl, cum_c - cum_r, NEG))
        kb = kst * beta_c
        kst_b = kst.astype(BF16)
        nmat = jnp.where(strict, _dot_nt(kb.astype(BF16), kst_b) * dmask, 0.0)
        amat = _inv_unit_lower(nmat, eye_f)
        ecum = jnp.exp(cum_c)
        rhs = jnp.concatenate([vst * beta_c, kb * ecum], axis=1)
        sol = _dot3(amat, rhs)
        qe = qst * ecum
        attn = _dot_nt(qst.astype(BF16), kst_b) * dmask
        cl_c = jnp.sum(jnp.where(last, cum_r, 0.0), axis=1, keepdims=True)
        kd = kst * jnp.exp(cl_c - cum_c)
        vnew, ointer = [], []
        for i in range(HS):
            rows = slice(i * L, (i + 1) * L)
            st_b = st_ref[hv0 + i].astype(BF16)
            both = _dot(jnp.concatenate([sol[rows, DH:], qe[rows]], axis=0).astype(BF16), st_b)
            vnew.append(sol[rows, :DH] - both[:L])
            ointer.append(both[L:])
        vnew_all = jnp.concatenate(vnew, axis=0)
        o = jnp.concatenate(ointer, axis=0) + _dot(attn.astype(BF16), vnew_all.astype(BF16))
        for i in range(HS):
            rows = slice(i * L, (i + 1) * L)
            hv = hv0 + i
            st_ref[hv] = st_ref[hv] * jnp.exp(cl_c[i * L:i * L + 1, :]) + _dot_tn(
                kd[rows].astype(BF16), vnew[i].astype(BF16))
            oi = o[rows]
            ms = jnp.mean(oi * oi, axis=1, keepdims=True)
            on = oi * lax.rsqrt(ms + EPS) * ng_ref[...]
            o_ref[:, hv * DH:(hv + 1) * DH] = (on * _silu(z_ref[:, hv * DH:(hv + 1) * DH])).astype(o_ref.dtype)


def gdn_core(proj, ba_raw, conv_w, a_log, dt_bias, norm_g, bsz, seq):
    L, DH = GDN_CHUNK, GDN_DH
    nv = GDN_STACK * GDN_STACKS
    t = proj.shape[0]
    hv_total = ba_raw.shape[1] // 2
    ng = hv_total // nv
    qw = (nv // 2) * DH
    vw = nv * DH
    nc = seq // L
    b_part = ba_raw[:, :hv_total].reshape(bsz, seq, ng, nv)
    a_part = ba_raw[:, hv_total:].reshape(bsz, seq, ng, nv)
    ba = jnp.concatenate([b_part, a_part], axis=-1).transpose(0, 2, 1, 3)
    row = lambda b, g, c: b * nc + c
    kq0 = ng
    v0 = 2 * ng * qw // vw
    return pl.pallas_call(
        _gdn_kernel,
        grid=(bsz, ng, nc),
        in_specs=[
            pl.BlockSpec((L, qw), lambda b, g, c: (row(b, g, c), g)),
            pl.BlockSpec((L, qw), lambda b, g, c: (row(b, g, c), kq0 + g)),
            pl.BlockSpec((L, vw), lambda b, g, c: (row(b, g, c), v0 + g)),
            pl.BlockSpec((L, vw), lambda b, g, c: (row(b, g, c), v0 + ng + g)),
            pl.BlockSpec((1, 1, L, 2 * nv), lambda b, g, c: (b, g, c, 0)),
            pl.BlockSpec((CONV_K, qw), lambda b, g, c: (0, g)),
            pl.BlockSpec((CONV_K, qw), lambda b, g, c: (0, kq0 + g)),
            pl.BlockSpec((CONV_K, vw), lambda b, g, c: (0, v0 + g)),
            pl.BlockSpec((1, 1, nv), lambda b, g, c: (g, 0, 0)),
            pl.BlockSpec((1, 1, nv), lambda b, g, c: (g, 0, 0)),
            pl.BlockSpec((1, DH), lambda b, g, c: (0, 0)),
        ],
        out_specs=pl.BlockSpec((L, vw), lambda b, g, c: (row(b, g, c), g)),
        out_shape=jax.ShapeDtypeStruct((t, hv_total * DH), BF16),
        scratch_shapes=[
            pltpu.VMEM((nv, DH, DH), F32),
            pltpu.VMEM((CARRY_ROWS, qw), F32),
            pltpu.VMEM((CARRY_ROWS, qw), F32),
            pltpu.VMEM((CARRY_ROWS, vw), F32),
        ],
        compiler_params=_cparams(("parallel", "parallel", "arbitrary")),
        name="gdn_core",
    )(proj, proj, proj, proj, ba, conv_w, conv_w, conv_w,
      a_log.reshape(ng, 1, nv), dt_bias.reshape(ng, 1, nv), norm_g.reshape(1, DH))


def gdn_layer(h, g_mix, w_in, conv_w, a_log, dt_bias, norm_g, w_out, bsz, seq):
    hv = a_log.shape[0]
    nmain = w_in.shape[1] - 2 * hv
    proj = norm_matmul(h, g_mix, w_in[:, :nmain].astype(BF16))
    ba_raw = norm_matmul(h, g_mix, _pad_cols(w_in[:, nmain:], LANES).astype(BF16))[:, :2 * hv]
    y = gdn_core(proj, ba_raw, conv_w, a_log, dt_bias, norm_g, bsz, seq)
    return matmul_residual(y, w_out.astype(BF16), h)


NSA_DK = 128
NSA_G = 4
NSA_HPG = 4
NSA_STRIDE = 16
NSA_CMP_BLOCK = 32
NSA_SEL = 64
NSA_TOP_N = 16
NSA_WINDOW = 512
NSA_TQ = 128
NSA_TK = 128
FORCE_BONUS = 1e4


def _cmp_kernel(x_ref, pa_ref, pb_ref, wa_ref, wb_ref, w2_ref, kg_ref, o_ref, acca_ref, accb_ref):
    kk = pl.program_id(2)

    @pl.when(kk == 0)
    def _():
        acca_ref[...] = jnp.zeros_like(acca_ref)
        accb_ref[...] = jnp.zeros_like(accb_ref)

    x = x_ref[0, 0]
    acca_ref[...] += _dot((x + pa_ref[0]).astype(BF16), wa_ref[0])
    accb_ref[...] += _dot((x + pb_ref[0]).astype(BF16), wb_ref[0])

    @pl.when(kk == pl.num_programs(2) - 1)
    def _():
        nrow = acca_ref.shape[0]
        pre = acca_ref[...] + pltpu.roll(accb_ref[...], nrow - 1, axis=0)
        y = _dot(_silu(pre).astype(BF16), w2_ref[0])
        is_k = pl.program_id(0) == 0
        for g in range(NSA_G):
            yg = y[:, g * NSA_DK:(g + 1) * NSA_DK]
            ms = jnp.mean(yg * yg, axis=1, keepdims=True)
            yn = yg * lax.rsqrt(ms + EPS) * kg_ref[...]
            o_ref[0, 0, :, g * NSA_DK:(g + 1) * NSA_DK] = jnp.where(is_k, yn, yg)


def nsa_compress(xkv, pos, w1, w2, kg0, *, tk=2048):
    _, bsz, ngrp, kdim = xkv.shape
    G, dk, half = NSA_G, NSA_DK, NSA_CMP_BLOCK // 2
    eye = jnp.eye(G, dtype=w1.dtype)

    def expand_w1(w):
        return jnp.einsum("clde,gh->clgdhe", w, eye).reshape(2, half * G * dk, G * dk).astype(BF16)

    def expand_pos(ps):
        return jnp.broadcast_to(ps[:, :, None, :], (2, half, G, dk)).reshape(2, 1, half * G * dk)

    wa, wb = expand_w1(w1[:, :half]), expand_w1(w1[:, half:])
    pa, pb = expand_pos(pos[:, :half]), expand_pos(pos[:, half:])
    w2bd = jnp.einsum("cde,gh->cgdhe", w2, eye).reshape(2, G * dk, G * dk).astype(BF16)
    tk = min(tk, kdim)
    assert kdim % tk == 0
    return pl.pallas_call(
        _cmp_kernel,
        grid=(2, bsz, kdim // tk),
        in_specs=[
            pl.BlockSpec((1, 1, ngrp, tk), lambda c, b, l: (c, b, 0, l)),
            pl.BlockSpec((1, 1, tk), lambda c, b, l: (c, 0, l)),
            pl.BlockSpec((1, 1, tk), lambda c, b, l: (c, 0, l)),
            pl.BlockSpec((1, tk, G * dk), lambda c, b, l: (c, l, 0)),
            pl.BlockSpec((1, tk, G * dk), lambda c, b, l: (c, l, 0)),
            pl.BlockSpec((1, G * dk, G * dk), lambda c, b, l: (c, 0, 0)),
            pl.BlockSpec((1, dk), lambda c, b, l: (0, 0)),
        ],
        out_specs=pl.BlockSpec((1, 1, ngrp, G * dk), lambda c, b, l: (c, b, 0, 0)),
        out_shape=jax.ShapeDtypeStruct((2, bsz, ngrp, G * dk), F32),
        scratch_shapes=[pltpu.VMEM((ngrp, G * dk), F32), pltpu.VMEM((ngrp, G * dk), F32)],
        compiler_params=_cparams(("parallel", "parallel", "arbitrary")),
        name="nsa_compress",
    )(xkv, pa, pb, wa, wb, w2bd, kg0.reshape(1, dk))


def _nsa_attn_kernel(q_ref, ks_ref, vs_ref, kw_ref, vw_ref, kc_ref, vc_ref, gate_ref, qg_ref, kg_ref, o_ref,
                     ksn_ref, vst_ref, kwn_ref, vwt_ref, kcb_ref, vct_ref, sel_ref, *, n_top):
    TQ, TK, DK, HPG = NSA_TQ, NSA_TK, NSA_DK, NSA_HPG
    i = pl.program_id(2)
    ntile = ksn_ref.shape[0]
    ncmp = kcb_ref.shape[0]
    nsel = sel_ref.shape[0]
    rows = HPG * TQ

    @pl.when(i == 0)
    def _():
        def prep(t, carry):
            sl = pl.ds(pl.multiple_of(t * TK, TK), TK)
            for src, gidx, dst in ((ks_ref, 1, ksn_ref), (kw_ref, 2, kwn_ref)):
                kt = src[sl, :]
                ms = jnp.mean(kt * kt, axis=1, keepdims=True)
                dst[t] = (kt * lax.rsqrt(ms + EPS) * kg_ref[gidx:gidx + 1, :]).astype(BF16)
            vst_ref[t] = vs_ref[sl, :].T.astype(BF16)
            vwt_ref[t] = vw_ref[sl, :].T.astype(BF16)
            return carry

        lax.fori_loop(0, ntile, prep, 0)
        kcb_ref[...] = kc_ref[0, 0].astype(BF16)
        vct_ref[...] = vc_ref[0, 0].T.astype(BF16)

    s0 = i * TQ
    q = q_ref[...]
    scale = DK ** -0.5
    qparts = []
    for r in range(HPG):
        qr = q[:, r * DK:(r + 1) * DK]
        ms = jnp.mean(qr * qr, axis=1, keepdims=True)
        qparts.append(qr * lax.rsqrt(ms + EPS) * (qg_ref[...] * scale))
    qrows = jnp.concatenate(qparts, axis=0).astype(BF16)
    tq = s0 + (_iota2((1, rows), 1) % TQ)

    sc = _dot_nt(kcb_ref[...], qrows)
    c_end = _iota2((ncmp, 1), 0) * NSA_STRIDE + (NSA_CMP_BLOCK - 1)
    maskc = c_end <= tq
    sc = jnp.where(maskc, sc, NEG)
    e = jnp.where(maskc, jnp.exp(sc - jnp.max(sc, axis=0, keepdims=True)), 0.0)
    pc = e / jnp.maximum(jnp.sum(e, axis=0, keepdims=True), 1e-30)
    oc = _dot(vct_ref[...], pc.astype(BF16))

    pcs = pc[:, 0:TQ]
    for r in range(1, HPG):
        pcs = pcs + pc[:, r * TQ:(r + 1) * TQ]
    jj = _iota2((nsel, ncmp), 0) * NSA_SEL
    cs = _iota2((nsel, ncmp), 1) * NSA_STRIDE
    overlap = ((cs < jj + NSA_SEL) & (cs + (NSA_CMP_BLOCK - 1) >= jj)).astype(F32)
    imp = _dot(overlap, pcs, precision=HIGHEST)
    jcol = _iota2((nsel, TQ), 0)
    cur = (s0 + _iota2((nsel, TQ), 1)) // NSA_SEL
    forced = (jcol == 0) | (jcol == cur) | (jcol == cur - 1)
    imp = jnp.where(jcol <= cur, imp + jnp.where(forced, FORCE_BONUS, 0.0), NEG)
    rank = jnp.zeros((nsel, TQ), F32)
    for j in range(nsel):
        vj = imp[j:j + 1, :]
        beats = (vj > imp) | ((vj == imp) & (jcol > j))
        rank = rank + beats.astype(F32)
    sel_ref[...] = (rank < n_top).astype(F32)

    kofs = _iota2((TK, 1), 0)
    blocks_per_tile = TK // NSA_SEL

    def attend(kt, carry, kn_ref, vt_ref, mask):
        m, l, acc = carry
        s = jnp.where(mask, _dot_nt(kn_ref[kt], qrows), NEG)
        m_new = jnp.maximum(m, jnp.max(s, axis=0, keepdims=True))
        alpha = jnp.exp(m - m_new)
        pexp = jnp.where(mask, jnp.exp(s - m_new), 0.0)
        l = alpha * l + jnp.sum(pexp, axis=0, keepdims=True)
        acc = alpha * acc + _dot(vt_ref[kt], pexp.astype(BF16))
        return m_new, l, acc

    def sel_body(kt, carry):
        parts = []
        for jb in range(blocks_per_tile):
            srow = sel_ref[pl.ds(kt * blocks_per_tile + jb, 1), :]
            srow = jnp.concatenate([srow] * HPG, axis=1)
            parts.append(jnp.broadcast_to(srow, (NSA_SEL, rows)))
        selm = jnp.concatenate(parts, axis=0) > 0.5
        kpos = kt * TK + kofs
        return attend(kt, carry, ksn_ref, vst_ref, selm & (kpos <= tq))

    def win_body(kt, carry):
        kpos = kt * TK + kofs
        return attend(kt, carry, kwn_ref, vwt_ref, (kpos <= tq) & (kpos > tq - NSA_WINDOW))

    init = (jnp.full((1, rows), NEG, F32), jnp.zeros((1, rows), F32), jnp.zeros((DK, rows), F32))
    _, l_s, acc_s = lax.fori_loop(0, i + 1, sel_body, init)
    lo = jnp.maximum(i - NSA_WINDOW // TK, 0)
    _, l_w, acc_w = lax.fori_loop(lo, i + 1, win_body, init)

    gt = _sigmoid(gate_ref[0, 0])

    def grow(c):
        return jnp.concatenate([gt[c * HPG + r:c * HPG + r + 1, :] for r in range(HPG)], axis=1)

    out = (grow(0) * oc + grow(1) * (acc_s / jnp.maximum(l_s, 1e-30))
           + grow(2) * (acc_w / jnp.maximum(l_w, 1e-30)))
    for r in range(HPG):
        o_ref[:, r * DK:(r + 1) * DK] = out[:, r * TQ:(r + 1) * TQ].T.astype(o_ref.dtype)


def nsa_attention(proj, kvc, gate, q_g, k_g, bsz, seq):
    G, HPG, DK, TQ, TK = NSA_G, NSA_HPG, NSA_DK, NSA_TQ, NSA_TK
    t = proj.shape[0]
    nq = seq // TQ
    ntile = seq // TK
    ncmp = kvc.shape[2]
    nsel = seq // NSA_SEL
    n_top = min(NSA_TOP_N, nsel)
    qw = HPG * DK
    kv0 = G * HPG

    def kvspec(which):
        return pl.BlockSpec((seq, DK), lambda b, g, i: (b, kv0 + which * G + g))

    return pl.pallas_call(
        functools.partial(_nsa_attn_kernel, n_top=n_top),
        grid=(bsz, G, nq),
        in_specs=[
            pl.BlockSpec((TQ, qw), lambda b, g, i: (b * nq + i, g)),
            kvspec(2), kvspec(3), kvspec(4), kvspec(5),
            pl.BlockSpec((1, 1, ncmp, DK), lambda b, g, i: (0, b, 0, g)),
            pl.BlockSpec((1, 1, ncmp, DK), lambda b, g, i: (1, b, 0, g)),
            pl.BlockSpec((1, 1, 4 * HPG, TQ), lambda b, g, i: (b, g, 0, i)),
            pl.BlockSpec((1, DK), lambda b, g, i: (0, 0)),
            pl.BlockSpec((3, DK), lambda b, g, i: (0, 0)),
        ],
        out_specs=pl.BlockSpec((TQ, qw), lambda b, g, i: (b * nq + i, g)),
        out_shape=jax.ShapeDtypeStruct((t, G * qw), BF16),
        scratch_shapes=[
            pltpu.VMEM((ntile, TK, DK), BF16),
            pltpu.VMEM((ntile, DK, TK), BF16),
            pltpu.VMEM((ntile, TK, DK), BF16),
            pltpu.VMEM((ntile, DK, TK), BF16),
            pltpu.VMEM((ncmp, DK), BF16),
            pltpu.VMEM((DK, ncmp), BF16),
            pltpu.VMEM((nsel, TQ), F32),
        ],
        compiler_params=_cparams(("parallel", "parallel", "arbitrary")),
        name="nsa_attention",
    )(proj, proj, proj, proj, proj, kvc, kvc, gate, q_g.reshape(1, DK), k_g)


def nsa_layer(h, g_mix, w_in, q_g, k_g, cmp_pos, cmp_w1, cmp_w2, w_out, bsz, seq):
    G, HPG, DK = NSA_G, NSA_HPG, NSA_DK
    H = G * HPG
    nmain = H * DK + 6 * G * DK
    proj = norm_matmul(h, g_mix, w_in[:, :nmain].astype(BF16))
    gate_raw = norm_matmul(h, g_mix, _pad_cols(w_in[:, nmain:], LANES).astype(BF16))[:, :3 * H]
    ngrp = seq // NSA_STRIDE
    kvw = G * DK
    xkv = jnp.stack([
        proj[:, H * DK:H * DK + kvw].reshape(bsz, ngrp, NSA_STRIDE * kvw),
        proj[:, H * DK + kvw:H * DK + 2 * kvw].reshape(bsz, ngrp, NSA_STRIDE * kvw),
    ])
    kvc = nsa_compress(xkv, cmp_pos, cmp_w1, cmp_w2, k_g[0])
    gate = gate_raw.reshape(bsz, seq, G, HPG, 3).transpose(0, 2, 4, 3, 1).reshape(bsz, G, 3 * HPG, seq)
    gate = jnp.pad(gate, ((0, 0), (0, 0), (0, HPG), (0, 0)))
    o = nsa_attention(proj, kvc, gate, q_g, k_g, bsz, seq)
    return matmul_residual(o, w_out.astype(BF16), h)


def kernel(x, p, norm_mix_g, norm_mlp_g, mlp_w1, mlp_w2, ple_norm_g, ple_w_gate, ple_w_proj, nsa_w_in, nsa_q_g, nsa_k_g, nsa_cmp_pos, nsa_cmp_w1, nsa_cmp_w2, nsa_w_out, m2_w_in, m2_conv_w, m2_conv_b, m2_dt_bias, m2_a_log, m2_d, m2_norm_g, m2_w_out, gdn_w_in, gdn_conv_w, gdn_a_log, gdn_dt_bias, gdn_norm_g, gdn_w_out, ml_w_in, ml_gate_b, ml_norm_g, ml_w_out):
    bsz, seq, d = x.shape
    h = x.reshape(bsz * seq, d)
    for i in range(p.shape[0]):
        kind, j = i % 4, i // 4
        if kind == 0:
            h = nsa_layer(h, norm_mix_g[i], nsa_w_in[j], nsa_q_g[j], nsa_k_g[j], nsa_cmp_pos[j], nsa_cmp_w1[j], nsa_cmp_w2[j], nsa_w_out[j], bsz, seq)
        elif kind == 1:
            h = mamba2_layer(h, norm_mix_g[i], m2_w_in[j], m2_conv_w[j], m2_conv_b[j], m2_dt_bias[j], m2_a_log[j], m2_d[j], m2_norm_g[j], m2_w_out[j], bsz, seq)
        elif kind == 2:
            h = gdn_layer(h, norm_mix_g[i], gdn_w_in[j], gdn_conv_w[j], gdn_a_log[j], gdn_dt_bias[j], gdn_norm_g[j], gdn_w_out[j], bsz, seq)
        elif kind == 3:
            h = mlstm_layer(h, norm_mix_g[i], ml_w_in[j], ml_gate_b[j], ml_norm_g[j], ml_w_out[j], bsz, seq)
        h = mlp_block(h, norm_mlp_g[i], mlp_w1[i].astype(BF16), mlp_w2[i].astype(BF16))
        h = ple_update(h, ple_norm_g[i], ple_w_gate[i].astype(BF16), p[i].reshape(bsz * seq, -1), ple_w_proj[i].astype(BF16))
    return h.reshape(bsz, seq, d)
```

```python
import functools
import math

import jax
import jax.numpy as jnp
from jax import lax
from jax.experimental import pallas as pl
from jax.experimental.pallas import tpu as pltpu

F32 = jnp.float32
BF16 = jnp.bfloat16
EPS = 1e-6
NEG = -1e30
HIGHEST = lax.Precision.HIGHEST

VMEM_LIMIT_BYTES = 56 * 1024 * 1024
LANES = 128


def _cparams(sem):
    return pltpu.CompilerParams(dimension_semantics=sem, vmem_limit_bytes=VMEM_LIMIT_BYTES)


def _silu(x):
    return x * (1.0 / (1.0 + jnp.exp(-x)))


def _sigmoid(x):
    return 1.0 / (1.0 + jnp.exp(-x))


def _softplus(x):
    return jnp.maximum(x, 0.0) + jnp.log(1.0 + jnp.exp(-jnp.abs(x)))


def _dot(a, b, **kw):
    return jnp.dot(a, b, preferred_element_type=F32, **kw)


def _dot_nt(a, b, **kw):
    return lax.dot_general(a, b, (((1,), (1,)), ((), ())), preferred_element_type=F32, **kw)


def _dot_tn(a, b, **kw):
    return lax.dot_general(a, b, (((0,), (0,)), ((), ())), preferred_element_type=F32, **kw)


def _norm_mm_kernel(x_ref, g_ref, w_ref, o_ref, xn_ref):
    @pl.when(pl.program_id(1) == 0)
    def _():
        x = x_ref[...]
        ms = jnp.mean(x * x, axis=-1, keepdims=True)
        xn_ref[...] = (x * lax.rsqrt(ms + EPS) * g_ref[...]).astype(BF16)

    o_ref[...] = _dot(xn_ref[...], w_ref[...]).astype(o_ref.dtype)


def _tile(dim, want):
    t = min(want, dim)
    while dim % t:
        t //= 2
    return t


def norm_matmul(x, g, w, *, out_dtype=F32, tm=1024, tn=1024):
    m, k = x.shape
    n = w.shape[1]
    tm, tn = _tile(m, tm), _tile(n, tn)
    return pl.pallas_call(
        _norm_mm_kernel,
        grid=(m // tm, n // tn),
        in_specs=[
            pl.BlockSpec((tm, k), lambda i, j: (i, 0)),
            pl.BlockSpec((1, k), lambda i, j: (0, 0)),
            pl.BlockSpec((k, tn), lambda i, j: (0, j)),
        ],
        out_specs=pl.BlockSpec((tm, tn), lambda i, j: (i, j)),
        out_shape=jax.ShapeDtypeStruct((m, n), out_dtype),
        scratch_shapes=[pltpu.VMEM((tm, k), BF16)],
        compiler_params=_cparams(("parallel", "arbitrary")),
        name="norm_matmul",
    )(x, g.reshape(1, k), w)


def _mm_res_kernel(x_ref, w_ref, r_ref, o_ref):
    o_ref[...] = r_ref[...] + _dot(x_ref[...], w_ref[...])


def matmul_residual(x, w, res, *, tm=1024, tn=512):
    m, k = x.shape
    n = w.shape[1]
    tm, tn = _tile(m, tm), _tile(n, tn)
    return pl.pallas_call(
        _mm_res_kernel,
        grid=(m // tm, n // tn),
        in_specs=[
            pl.BlockSpec((tm, k), lambda i, j: (i, 0)),
            pl.BlockSpec((k, tn), lambda i, j: (0, j)),
            pl.BlockSpec((tm, tn), lambda i, j: (i, j)),
        ],
        out_specs=pl.BlockSpec((tm, tn), lambda i, j: (i, j)),
        out_shape=jax.ShapeDtypeStruct((m, n), F32),
        compiler_params=_cparams(("parallel", "arbitrary")),
        name="matmul_residual",
    )(x, w, res)


def _ple_kernel(x_ref, g_ref, wg_ref, p_ref, wp_ref, r_ref, o_ref, xn_ref):
    @pl.when(pl.program_id(1) == 0)
    def _():
        x = x_ref[...]
        ms = jnp.mean(x * x, axis=-1, keepdims=True)
        xn_ref[...] = (x * lax.rsqrt(ms + EPS) * g_ref[...]).astype(BF16)

    gate = _sigmoid(_dot(xn_ref[...], wg_ref[...]))
    proj = _dot(p_ref[...].astype(BF16), wp_ref[...])
    o_ref[...] = r_ref[...] + gate * proj


def ple_update(h, g, wg, p, wp, *, tm=1024, tn=512):
    m, k = h.shape
    n = wg.shape[1]
    kp = p.shape[1]
    tm, tn = _tile(m, tm), _tile(n, tn)
    return pl.pallas_call(
        _ple_kernel,
        grid=(m // tm, n // tn),
        in_specs=[
            pl.BlockSpec((tm, k), lambda i, j: (i, 0)),
            pl.BlockSpec((1, k), lambda i, j: (0, 0)),
            pl.BlockSpec((k, tn), lambda i, j: (0, j)),
            pl.BlockSpec((tm, kp), lambda i, j: (i, 0)),
            pl.BlockSpec((kp, tn), lambda i, j: (0, j)),
            pl.BlockSpec((tm, tn), lambda i, j: (i, j)),
        ],
        out_specs=pl.BlockSpec((tm, tn), lambda i, j: (i, j)),
        out_shape=jax.ShapeDtypeStruct((m, n), F32),
        scratch_shapes=[pltpu.VMEM((tm, k), BF16)],
        compiler_params=_cparams(("parallel", "arbitrary")),
        name="ple_update",
    )(h, g.reshape(1, k), wg, p, wp, h)


def _mlp_kernel(x_ref, g_ref, w1_ref, w2_ref, o_ref, xn_ref):
    @pl.when(pl.program_id(1) == 0)
    def _():
        x = x_ref[...]
        ms = jnp.mean(x * x, axis=-1, keepdims=True)
        xn_ref[...] = (x * lax.rsqrt(ms + EPS) * g_ref[...]).astype(BF16)
        o_ref[...] = x

    a = jnp.maximum(_dot(xn_ref[...], w1_ref[...]), 0.0)
    o_ref[...] += _dot((a * a).astype(BF16), w2_ref[...])


def mlp_block(h, g, w1, w2, *, tm=1024, tf=512):
    m, k = h.shape
    f = w1.shape[1]
    tm, tf = _tile(m, tm), _tile(f, tf)
    return pl.pallas_call(
        _mlp_kernel,
        grid=(m // tm, f // tf),
        in_specs=[
            pl.BlockSpec((tm, k), lambda i, j: (i, 0)),
            pl.BlockSpec((1, k), lambda i, j: (0, 0)),
            pl.BlockSpec((k, tf), lambda i, j: (0, j)),
            pl.BlockSpec((tf, k), lambda i, j: (j, 0)),
        ],
        out_specs=pl.BlockSpec((tm, k), lambda i, j: (i, 0)),
        out_shape=jax.ShapeDtypeStruct((m, k), F32),
        scratch_shapes=[pltpu.VMEM((tm, k), BF16)],
        compiler_params=_cparams(("parallel", "arbitrary")),
        name="mlp_block",
    )(h, g.reshape(1, k), w1, w2)


def _pad_cols(w, n):
    return jnp.pad(w, ((0, 0), (0, n - w.shape[1])))


def _iota2(shape, dim):
    return lax.broadcasted_iota(jnp.int32, shape, dim)


def _col_to_row(col, eye):
    return jnp.sum(jnp.where(eye, col, 0.0), axis=0, keepdims=True)


def _log_sigmoid(x):
    return jnp.minimum(x, 0.0) - jnp.log(1.0 + jnp.exp(-jnp.abs(x)))


ML_HEADS = 4
ML_CHUNK = 64


def _mlstm_kernel(q_ref, k_ref, v_ref, og_ref, gt_ref, gb_ref, ng_ref, o_ref, c_ref, n_ref, m_ref, *, dq, dv):
    L = ML_CHUNK

    @pl.when(pl.program_id(1) == 0)
    def _():
        c_ref[...] = jnp.zeros_like(c_ref)
        n_ref[...] = jnp.zeros_like(n_ref)
        m_ref[...] = jnp.zeros_like(m_ref)

    row = _iota2((L, L), 0)
    col = _iota2((L, L), 1)
    incl = row >= col
    eye = row == col
    gates = gt_ref[...] + gb_ref[...]
    scale = dq ** -0.5
    for h in range(ML_HEADS):
        q = q_ref[:, h * dq:(h + 1) * dq] * scale
        k = k_ref[:, h * dq:(h + 1) * dq]
        v = v_ref[:, h * dv:(h + 1) * dv]
        li_c = gates[:, h:h + 1]
        lf_c = _log_sigmoid(gates[:, ML_HEADS + h:ML_HEADS + h + 1])
        li_r = _col_to_row(li_c, eye)
        lf_r = _col_to_row(lf_c, eye)
        b_c = jnp.sum(jnp.where(incl, lf_r, 0.0), axis=1, keepdims=True)
        b_r = _col_to_row(b_c, eye)
        m_prev = m_ref[h]
        dlog = jnp.where(incl, b_c - b_r + li_r, NEG)
        inter = b_c + m_prev
        m_t = jnp.maximum(inter, jnp.max(dlog, axis=1, keepdims=True))
        w_inter = jnp.exp(inter - m_t)
        qb = q.astype(BF16)
        kb = k.astype(BF16)
        vb = v.astype(BF16)
        s = _dot_nt(qb, kb) * jnp.where(incl, jnp.exp(dlog - m_t), 0.0)
        c_prev = c_ref[h]
        num = _dot(s.astype(BF16), vb) + w_inter * _dot(qb, c_prev.astype(BF16))
        qn = jnp.sum(q * n_ref[h], axis=1, keepdims=True)
        den = jnp.sum(s, axis=1, keepdims=True) + w_inter * qn
        hc = num / jnp.maximum(jnp.abs(den), jnp.exp(-m_t))
        tot = b_c[L - 1:L, :]
        ulog_c = tot - b_c + li_c
        m_new = jnp.maximum(tot + m_prev, jnp.max(ulog_c, axis=0, keepdims=True))
        wu_c = jnp.exp(ulog_c - m_new)
        dec = jnp.exp(tot + m_prev - m_new)
        kw = k * wu_c
        c_ref[h] = dec * c_prev + _dot_tn(kw.astype(BF16), vb)
        n_ref[h] = dec * n_ref[h] + jnp.sum(kw, axis=0, keepdims=True)
        m_ref[h] = m_new
        ms = jnp.mean(hc * hc, axis=1, keepdims=True)
        hn = hc * lax.rsqrt(ms + EPS) * ng_ref[:, h * dv:(h + 1) * dv]
        o_ref[:, h * dv:(h + 1) * dv] = (_sigmoid(og_ref[:, h * dv:(h + 1) * dv]) * hn).astype(o_ref.dtype)


def mlstm_core(proj, gates, gate_b, norm_g, bsz, seq):
    H, L = ML_HEADS, ML_CHUNK
    t = proj.shape[0]
    dq = proj.shape[1] // (6 * H)
    dv = 2 * dq
    nc = seq // L
    gb = jnp.pad(gate_b, (0, LANES - gate_b.shape[0])).reshape(1, LANES)
    rowmap = lambda b, c: b * nc + c
    return pl.pallas_call(
        functools.partial(_mlstm_kernel, dq=dq, dv=dv),
        grid=(bsz, nc),
        in_specs=[
            pl.BlockSpec((L, H * dq), lambda b, c: (rowmap(b, c), 0)),
            pl.BlockSpec((L, H * dq), lambda b, c: (rowmap(b, c), 1)),
            pl.BlockSpec((L, H * dv), lambda b, c: (rowmap(b, c), 1)),
            pl.BlockSpec((L, H * dv), lambda b, c: (rowmap(b, c), 2)),
            pl.BlockSpec((L, LANES), lambda b, c: (rowmap(b, c), 0)),
            pl.BlockSpec((1, LANES), lambda b, c: (0, 0)),
            pl.BlockSpec((1, H * dv), lambda b, c: (0, 0)),
        ],
        out_specs=pl.BlockSpec((L, H * dv), lambda b, c: (rowmap(b, c), 0)),
        out_shape=jax.ShapeDtypeStruct((t, H * dv), BF16),
        scratch_shapes=[
            pltpu.VMEM((H, dq, dv), F32),
            pltpu.VMEM((H, 1, dq), F32),
            pltpu.VMEM((H, 1, 1), F32),
        ],
        compiler_params=_cparams(("parallel", "arbitrary")),
        name="mlstm_core",
    )(proj, proj, proj, proj, gates, gb, norm_g.reshape(1, H * dv))


def mlstm_layer(h, g_mix, w_in, gate_b, norm_g, w_out, bsz, seq):
    nmain = w_in.shape[1] - 2 * ML_HEADS
    proj = norm_matmul(h, g_mix, w_in[:, :nmain].astype(BF16))
    gates = norm_matmul(h, g_mix, _pad_cols(w_in[:, nmain:], LANES).astype(BF16))
    y = mlstm_core(proj, gates, gate_b, norm_g, bsz, seq)
    return matmul_residual(y, w_out.astype(BF16), h)


M2_CHUNK = 128
M2_GROUPS = 8
M2_STATE = 128
M2_HEAD_DIM = 64
CONV_K = 4
CARRY_ROWS = 8


def _causal_conv_silu(x_raw, carry_ref, w, b):
    n = x_raw.shape[0]
    xe = jnp.concatenate([carry_ref[...], x_raw], axis=0)
    acc = x_raw * w[CONV_K - 1:CONV_K, :]
    if b is not None:
        acc = acc + b
    for k in range(CONV_K - 1):
        sh = CONV_K - 1 - k
        acc = acc + pltpu.roll(xe, sh, axis=0)[CARRY_ROWS:, :] * w[k:k + 1, :]
    carry_ref[...] = x_raw[n - CARRY_ROWS:, :]
    return _silu(acc)


def _ssd_kernel(z_ref, xs_ref, bm_ref, cm_ref, dt_ref, cwx_ref, cwb_ref, cwc_ref, cbx_ref, cbb_ref, cbc_ref,
                dtb_ref, alog_ref, dsk_ref, ng_ref, o_ref, st_ref, cx_ref, cbm_ref, ccm_ref, xs_s, y_s):
    L, P = M2_CHUNK, M2_HEAD_DIM
    R = st_ref.shape[0]

    @pl.when(pl.program_id(2) == 0)
    def _():
        st_ref[...] = jnp.zeros_like(st_ref)
        cx_ref[...] = jnp.zeros_like(cx_ref)
        cbm_ref[...] = jnp.zeros_like(cbm_ref)
        ccm_ref[...] = jnp.zeros_like(ccm_ref)

    xs = _causal_conv_silu(xs_ref[...], cx_ref, cwx_ref[...], cbx_ref[...])
    bm = _causal_conv_silu(bm_ref[...], cbm_ref, cwb_ref[...], cbb_ref[...])
    cm = _causal_conv_silu(cm_ref[...], ccm_ref, cwc_ref[...], cbc_ref[...])
    xs_s[...] = xs
    dt = _softplus(dt_ref[0, 0] + dtb_ref[0])
    lac = dt * (-jnp.exp(alog_ref[0]))
    row = _iota2((L, L), 0)
    col = _iota2((L, L), 1)
    tri = row >= col
    eye = row == col
    cb = _dot_nt(cm.astype(BF16), bm.astype(BF16))
    for r in range(R):
        dt_c = dt[:, r:r + 1]
        dt_r = _col_to_row(dt_c, eye)
        lac_r = _col_to_row(lac[:, r:r + 1], eye)
        cum_c = jnp.sum(jnp.where(tri, lac_r, 0.0), axis=1, keepdims=True)
        cum_r = _col_to_row(cum_c, eye)
        dec = jnp.exp(jnp.where(tri, cum_c - cum_r, NEG))
        mm = cb * dec * dt_r
        ce = cm * jnp.exp(cum_c)
        lhs = jnp.concatenate([mm, ce], axis=1).astype(BF16)
        xs_r = xs_s[:, r * P:(r + 1) * P]
        st = st_ref[r]
        rhs = jnp.concatenate([xs_r, st], axis=0).astype(BF16)
        y_s[:, r * P:(r + 1) * P] = _dot(lhs, rhs)
        cl = cum_c[L - 1:L, :]
        wgt = dt_c * jnp.exp(cl - cum_c)
        st_ref[r] = st * jnp.exp(cl) + _dot_tn((bm * wgt).astype(BF16), xs_r.astype(BF16))
    y = y_s[...] + dsk_ref[...] * xs
    y = y * _silu(z_ref[...])
    ms = jnp.mean(y * y, axis=1, keepdims=True)
    o_ref[...] = (y * lax.rsqrt(ms + EPS) * ng_ref[...]).astype(o_ref.dtype)


def ssd_core(proj, dt_raw, conv_w, conv_b, dt_bias, a_log, d_skip, norm_g, bsz, seq):
    L, G, N, P = M2_CHUNK, M2_GROUPS, M2_STATE, M2_HEAD_DIM
    t = proj.shape[0]
    heads = dt_raw.shape[1]
    R = heads // G
    di = heads * P
    gw = di // G
    nc = seq // L
    dtc = dt_raw.reshape(bsz, seq, G, R).transpose(0, 2, 1, 3)
    cw = conv_w
    cbias = conv_b.reshape(1, -1)
    row = lambda b, g, c: b * nc + c
    nzx = di // gw
    nb0 = 2 * di // N
    cwb0 = di // N
    return pl.pallas_call(
        _ssd_kernel,
        grid=(bsz, G, nc),
        in_specs=[
            pl.BlockSpec((L, gw), lambda b, g, c: (row(b, g, c), g)),
            pl.BlockSpec((L, gw), lambda b, g, c: (row(b, g, c), nzx + g)),
            pl.BlockSpec((L, N), lambda b, g, c: (row(b, g, c), nb0 + g)),
            pl.BlockSpec((L, N), lambda b, g, c: (row(b, g, c), nb0 + G + g)),
            pl.BlockSpec((1, 1, L, R), lambda b, g, c: (b, g, c, 0)),
            pl.BlockSpec((CONV_K, gw), lambda b, g, c: (0, g)),
            pl.BlockSpec((CONV_K, N), lambda b, g, c: (0, cwb0 + g)),
            pl.BlockSpec((CONV_K, N), lambda b, g, c: (0, cwb0 + G + g)),
            pl.BlockSpec((1, gw), lambda b, g, c: (0, g)),
            pl.BlockSpec((1, N), lambda b, g, c: (0, cwb0 + g)),
            pl.BlockSpec((1, N), lambda b, g, c: (0, cwb0 + G + g)),
            pl.BlockSpec((1, 1, R), lambda b, g, c: (g, 0, 0)),
            pl.BlockSpec((1, 1, R), lambda b, g, c: (g, 0, 0)),
            pl.BlockSpec((1, gw), lambda b, g, c: (0, g)),
            pl.BlockSpec((1, gw), lambda b, g, c: (0, g)),
        ],
        out_specs=pl.BlockSpec((L, gw), lambda b, g, c: (row(b, g, c), g)),
        out_shape=jax.ShapeDtypeStruct((t, di), BF16),
        scratch_shapes=[
            pltpu.VMEM((R, N, P), F32),
            pltpu.VMEM((CARRY_ROWS, gw), F32),
            pltpu.VMEM((CARRY_ROWS, N), F32),
            pltpu.VMEM((CARRY_ROWS, N), F32),
            pltpu.VMEM((L, gw), F32),
            pltpu.VMEM((L, gw), F32),
        ],
        compiler_params=_cparams(("parallel", "parallel", "arbitrary")),
        name="ssd_core",
    )(proj, proj, proj, proj, dtc, cw, cw, cw, cbias, cbias, cbias,
      dt_bias.reshape(G, 1, R), a_log.reshape(G, 1, R),
      jnp.repeat(d_skip, P).reshape(1, di), norm_g.reshape(1, di))


def mamba2_layer(h, g_mix, w_in, conv_w, conv_b, dt_bias, a_log, d_skip, norm_g, w_out, bsz, seq):
    heads = dt_bias.shape[0]
    nmain = w_in.shape[1] - heads
    proj = norm_matmul(h, g_mix, w_in[:, :nmain].astype(BF16))
    dt_raw = norm_matmul(h, g_mix, _pad_cols(w_in[:, nmain:], LANES).astype(BF16))[:, :heads]
    y = ssd_core(proj, dt_raw, conv_w, conv_b, dt_bias, a_log, d_skip, norm_g, bsz, seq)
    return matmul_residual(y, w_out.astype(BF16), h)


GDN_CHUNK = 64
GDN_DH = 128
GDN_STACK = 4
GDN_STACKS = 2


def _inv_unit_lower(nmat, eye_f):
    a = eye_f - nmat
    pw_b = nmat.astype(BF16)
    for _ in range(5):
        pw = _dot(pw_b, pw_b)
        pw_b = pw.astype(BF16)
        a = a + _dot(a.astype(BF16), pw_b)
    return a


def _l2norm(x):
    return x * lax.rsqrt(jnp.sum(x * x, axis=1, keepdims=True) + EPS)


def _gdn_kernel(q_ref, k_ref, v_ref, z_ref, ba_ref, cwq_ref, cwk_ref, cwv_ref, alog_ref, dtb_ref, ng_ref,
                o_ref, st_ref, cq_ref, ck_ref, cv_ref):
    L, DH, HS = GDN_CHUNK, GDN_DH, GDN_STACK
    nv = GDN_STACK * GDN_STACKS
    SL = HS * L

    @pl.when(pl.program_id(2) == 0)
    def _():
        st_ref[...] = jnp.zeros_like(st_ref)
        cq_ref[...] = jnp.zeros_like(cq_ref)
        ck_ref[...] = jnp.zeros_like(ck_ref)
        cv_ref[...] = jnp.zeros_like(cv_ref)

    q = _causal_conv_silu(q_ref[...], cq_ref, cwq_ref[...], None)
    k = _causal_conv_silu(k_ref[...], ck_ref, cwk_ref[...], None)
    v = _causal_conv_silu(v_ref[...], cv_ref, cwv_ref[...], None)
    ba = ba_ref[0, 0]
    beta = _sigmoid(ba[:, :nv])
    gg = -jnp.exp(alog_ref[0]) * _softplus(ba[:, nv:2 * nv] + dtb_ref[0])
    qn = [_l2norm(q[:, i * DH:(i + 1) * DH]) * DH ** -0.5 for i in range(nv // 2)]
    kn = [_l2norm(k[:, i * DH:(i + 1) * DH]) for i in range(nv // 2)]

    ri = _iota2((SL, SL), 0)
    ci = _iota2((SL, SL), 1)
    same = (ri // L) == (ci // L)
    incl = same & (ri >= ci)
    strict = same & (ri > ci)
    eye = ri == ci
    eye_f = eye.astype(F32)
    last = same & ((ci % L) == L - 1)

    for s in range(GDN_STACKS):
        hv0 = s * HS
        kst = jnp.concatenate([kn[(hv0 + i) // 2] for i in range(HS)], axis=0)
        qst = jnp.concatenate([qn[(hv0 + i) // 2] for i in range(HS)], axis=0)
        vst = jnp.concatenate([v[:, (hv0 + i) * DH:(hv0 + i + 1) * DH] for i in range(HS)], axis=0)
        beta_c = jnp.concatenate([beta[:, hv0 + i:hv0 + i + 1] for i in range(HS)], axis=0)
        g_c = jnp.concatenate([gg[:, hv0 + i:hv0 + i + 1] for i in range(HS)], axis=0)
        g_r = _col_to_row(g_c, eye)
        cum_c = jnp.sum(jnp.where(incl, g_r, 0.0), axis=1, keepdims=True)
        cum_r = _col_to_row(cum_c, eye)
        dmask = jnp.exp(jnp.where(incl, cum_c - cum_r, NEG))
        kb = kst * beta_c
        kst_b = kst.astype(BF16)
        nmat = jnp.where(strict, _dot_nt(kb.astype(BF16), kst_b) * dmask, 0.0)
        amat = _inv_unit_lower(nmat, eye_f)
        ecum = jnp.exp(cum_c)
        rhs = jnp.concatenate([vst * beta_c, kb * ecum], axis=1)
        sol = _dot(amat.astype(BF16), rhs.astype(BF16))
        qe = qst * ecum
        attn = _dot_nt(qst.astype(BF16), kst_b) * dmask
        cl_c = jnp.sum(jnp.where(last, cum_r, 0.0), axis=1, keepdims=True)
        kd = kst * jnp.exp(cl_c - cum_c)
        vnew, ointer = [], []
        for i in range(HS):
            rows = slice(i * L, (i + 1) * L)
            st_b = st_ref[hv0 + i].astype(BF16)
            both = _dot(jnp.concatenate([sol[rows, DH:], qe[rows]], axis=0).astype(BF16), st_b)
            vnew.append(sol[rows, :DH] - both[:L])
            ointer.append(both[L:])
        vnew_all = jnp.concatenate(vnew, axis=0)
        o = jnp.concatenate(ointer, axis=0) + _dot(attn.astype(BF16), vnew_all.astype(BF16))
        for i in range(HS):
            rows = slice(i * L, (i + 1) * L)
            hv = hv0 + i
            st_ref[hv] = st_ref[hv] * jnp.exp(cl_c[i * L:i * L + 1, :]) + _dot_tn(
                kd[rows].astype(BF16), vnew[i].astype(BF16))
            oi = o[rows]
            ms = jnp.mean(oi * oi, axis=1, keepdims=True)
            on = oi * lax.rsqrt(ms + EPS) * ng_ref[...]
            o_ref[:, hv * DH:(hv + 1) * DH] = (on * _silu(z_ref[:, hv * DH:(hv + 1) * DH])).astype(o_ref.dtype)


def gdn_core(proj, ba_raw, conv_w, a_log, dt_bias, norm_g, bsz, seq):
    L, DH = GDN_CHUNK, GDN_DH
    nv = GDN_STACK * GDN_STACKS
    t = proj.shape[0]
    hv_total = ba_raw.shape[1] // 2
    ng = hv_total // nv
    qw = (nv // 2) * DH
    vw = nv * DH
    nc = seq // L
    b_part = ba_raw[:, :hv_total].reshape(bsz, seq, ng, nv)
    a_part = ba_raw[:, hv_total:].reshape(bsz, seq, ng, nv)
    ba = jnp.concatenate([b_part, a_part], axis=-1).transpose(0, 2, 1, 3)
    row = lambda b, g, c: b * nc + c
    kq0 = ng
    v0 = 2 * ng * qw // vw
    return pl.pallas_call(
        _gdn_kernel,
        grid=(bsz, ng, nc),
        in_specs=[
            pl.BlockSpec((L, qw), lambda b, g, c: (row(b, g, c), g)),
            pl.BlockSpec((L, qw), lambda b, g, c: (row(b, g, c), kq0 + g)),
            pl.BlockSpec((L, vw), lambda b, g, c: (row(b, g, c), v0 + g)),
            pl.BlockSpec((L, vw), lambda b, g, c: (row(b, g, c), v0 + ng + g)),
            pl.BlockSpec((1, 1, L, 2 * nv), lambda b, g, c: (b, g, c, 0)),
            pl.BlockSpec((CONV_K, qw), lambda b, g, c: (0, g)),
            pl.BlockSpec((CONV_K, qw), lambda b, g, c: (0, kq0 + g)),
            pl.BlockSpec((CONV_K, vw), lambda b, g, c: (0, v0 + g)),
            pl.BlockSpec((1, 1, nv), lambda b, g, c: (g, 0, 0)),
            pl.BlockSpec((1, 1, nv), lambda b, g, c: (g, 0, 0)),
            pl.BlockSpec((1, DH), lambda b, g, c: (0, 0)),
        ],
        out_specs=pl.BlockSpec((L, vw), lambda b, g, c: (row(b, g, c), g)),
        out_shape=jax.ShapeDtypeStruct((t, hv_total * DH), BF16),
        scratch_shapes=[
            pltpu.VMEM((nv, DH, DH), F32),
            pltpu.VMEM((CARRY_ROWS, qw), F32),
            pltpu.VMEM((CARRY_ROWS, qw), F32),
            pltpu.VMEM((CARRY_ROWS, vw), F32),
        ],
        compiler_params=_cparams(("parallel", "parallel", "arbitrary")),
        name="gdn_core",
    )(proj, proj, proj, proj, ba, conv_w, conv_w, conv_w,
      a_log.reshape(ng, 1, nv), dt_bias.reshape(ng, 1, nv), norm_g.reshape(1, DH))


def gdn_layer(h, g_mix, w_in, conv_w, a_log, dt_bias, norm_g, w_out, bsz, seq):
    hv = a_log.shape[0]
    nmain = w_in.shape[1] - 2 * hv
    proj = norm_matmul(h, g_mix, w_in[:, :nmain].astype(BF16))
    ba_raw = norm_matmul(h, g_mix, _pad_cols(w_in[:, nmain:], LANES).astype(BF16))[:, :2 * hv]
    y = gdn_core(proj, ba_raw, conv_w, a_log, dt_bias, norm_g, bsz, seq)
    return matmul_residual(y, w_out.astype(BF16), h)


NSA_DK = 128
NSA_G = 4
NSA_HPG = 4
NSA_STRIDE = 16
NSA_CMP_BLOCK = 32
NSA_SEL = 64
NSA_TOP_N = 16
NSA_WINDOW = 512
NSA_TQ = 128
NSA_TKS = 512
NSA_TKW = 128
FORCE_BONUS = 1e4


def _cmp_kernel(x_ref, pa_ref, pb_ref, wa_ref, wb_ref, w2_ref, kg_ref, o_ref, acca_ref, accb_ref):
    kk = pl.program_id(2)

    @pl.when(kk == 0)
    def _():
        acca_ref[...] = jnp.zeros_like(acca_ref)
        accb_ref[...] = jnp.zeros_like(accb_ref)

    x = x_ref[0, 0]
    acca_ref[...] += _dot((x + pa_ref[0]).astype(BF16), wa_ref[0])
    accb_ref[...] += _dot((x + pb_ref[0]).astype(BF16), wb_ref[0])

    @pl.when(kk == pl.num_programs(2) - 1)
    def _():
        nrow = acca_ref.shape[0]
        pre = acca_ref[...] + pltpu.roll(accb_ref[...], nrow - 1, axis=0)
        y = _dot(_silu(pre).astype(BF16), w2_ref[0])
        is_k = pl.program_id(0) == 0
        for g in range(NSA_G):
            yg = y[:, g * NSA_DK:(g + 1) * NSA_DK]
            ms = jnp.mean(yg * yg, axis=1, keepdims=True)
            yn = yg * lax.rsqrt(ms + EPS) * kg_ref[...]
            o_ref[0, 0, :, g * NSA_DK:(g + 1) * NSA_DK] = jnp.where(is_k, yn, yg)


def nsa_compress(xkv, pos, w1, w2, kg0, *, tk=2048):
    _, bsz, ngrp, kdim = xkv.shape
    G, dk, half = NSA_G, NSA_DK, NSA_CMP_BLOCK // 2
    eye = jnp.eye(G, dtype=w1.dtype)

    def expand_w1(w):
        return jnp.einsum("clde,gh->clgdhe", w, eye).reshape(2, half * G * dk, G * dk).astype(BF16)

    def expand_pos(ps):
        return jnp.broadcast_to(ps[:, :, None, :], (2, half, G, dk)).reshape(2, 1, half * G * dk)

    wa, wb = expand_w1(w1[:, :half]), expand_w1(w1[:, half:])
    pa, pb = expand_pos(pos[:, :half]), expand_pos(pos[:, half:])
    w2bd = jnp.einsum("cde,gh->cgdhe", w2, eye).reshape(2, G * dk, G * dk).astype(BF16)
    tk = min(tk, kdim)
    assert kdim % tk == 0
    return pl.pallas_call(
        _cmp_kernel,
        grid=(2, bsz, kdim // tk),
        in_specs=[
            pl.BlockSpec((1, 1, ngrp, tk), lambda c, b, l: (c, b, 0, l)),
            pl.BlockSpec((1, 1, tk), lambda c, b, l: (c, 0, l)),
            pl.BlockSpec((1, 1, tk), lambda c, b, l: (c, 0, l)),
            pl.BlockSpec((1, tk, G * dk), lambda c, b, l: (c, l, 0)),
            pl.BlockSpec((1, tk, G * dk), lambda c, b, l: (c, l, 0)),
            pl.BlockSpec((1, G * dk, G * dk), lambda c, b, l: (c, 0, 0)),
            pl.BlockSpec((1, dk), lambda c, b, l: (0, 0)),
        ],
        out_specs=pl.BlockSpec((1, 1, ngrp, G * dk), lambda c, b, l: (c, b, 0, 0)),
        out_shape=jax.ShapeDtypeStruct((2, bsz, ngrp, G * dk), F32),
        scratch_shapes=[pltpu.VMEM((ngrp, G * dk), F32), pltpu.VMEM((ngrp, G * dk), F32)],
        compiler_params=_cparams(("parallel", "parallel", "arbitrary")),
        name="nsa_compress",
    )(xkv, pa, pb, wa, wb, w2bd, kg0.reshape(1, dk))


def _nsa_attn_kernel(q_ref, ks_ref, vs_ref, kw_ref, vw_ref, kc_ref, vc_ref, gate_ref, qg_ref, kg_ref, o_ref,
                     ksn_ref, vst_ref, kwn_ref, vwt_ref, kcb_ref, vct_ref, sel_ref, *, n_top):
    TQ, TS, TW, DK, HPG = NSA_TQ, NSA_TKS, NSA_TKW, NSA_DK, NSA_HPG
    i = pl.program_id(2)
    ncmp = kcb_ref.shape[0]
    nsel = sel_ref.shape[0]
    rows = HPG * TQ

    @pl.when(i == 0)
    def _():
        def norm_key(kt, gidx):
            ms = jnp.mean(kt * kt, axis=1, keepdims=True)
            return (kt * lax.rsqrt(ms + EPS) * kg_ref[gidx:gidx + 1, :]).astype(BF16)

        def prep_sel(t, carry):
            sl = pl.ds(pl.multiple_of(t * TS, TS), TS)
            ksn_ref[t] = norm_key(ks_ref[sl, :], 1)
            vst_ref[t] = vs_ref[sl, :].T.astype(BF16)
            return carry

        def prep_win(t, carry):
            sl = pl.ds(pl.multiple_of(t * TW, TW), TW)
            kwn_ref[t] = norm_key(kw_ref[sl, :], 2)
            vwt_ref[t] = vw_ref[sl, :].T.astype(BF16)
            return carry

        lax.fori_loop(0, ksn_ref.shape[0], prep_sel, 0)
        lax.fori_loop(0, kwn_ref.shape[0], prep_win, 0)
        kcb_ref[...] = kc_ref[0, 0].astype(BF16)
        vct_ref[...] = vc_ref[0, 0].T.astype(BF16)

    s0 = i * TQ
    q = q_ref[...]
    scale = DK ** -0.5
    qparts = []
    for r in range(HPG):
        qr = q[:, r * DK:(r + 1) * DK]
        ms = jnp.mean(qr * qr, axis=1, keepdims=True)
        qparts.append(qr * lax.rsqrt(ms + EPS) * (qg_ref[...] * scale))
    qrows = jnp.concatenate(qparts, axis=0).astype(BF16)
    tq = s0 + (_iota2((1, rows), 1) % TQ)

    sc = _dot_nt(kcb_ref[...], qrows)
    c_end = _iota2((ncmp, 1), 0) * NSA_STRIDE + (NSA_CMP_BLOCK - 1)
    maskc = c_end <= tq
    sc = jnp.where(maskc, sc, NEG)
    e = jnp.where(maskc, jnp.exp(sc - jnp.max(sc, axis=0, keepdims=True)), 0.0)
    pc = e / jnp.maximum(jnp.sum(e, axis=0, keepdims=True), 1e-30)
    oc = _dot(vct_ref[...], pc.astype(BF16))

    pcs = pc[:, 0:TQ]
    for r in range(1, HPG):
        pcs = pcs + pc[:, r * TQ:(r + 1) * TQ]
    jj = _iota2((nsel, ncmp), 0) * NSA_SEL
    cs = _iota2((nsel, ncmp), 1) * NSA_STRIDE
    overlap = ((cs < jj + NSA_SEL) & (cs + (NSA_CMP_BLOCK - 1) >= jj)).astype(F32)
    imp = _dot(overlap, pcs, precision=HIGHEST)
    jcol = _iota2((nsel, TQ), 0)
    cur = (s0 + _iota2((nsel, TQ), 1)) // NSA_SEL
    forced = (jcol == 0) | (jcol == cur) | (jcol == cur - 1)
    imp = jnp.where(jcol <= cur, imp + jnp.where(forced, FORCE_BONUS, 0.0), NEG)
    rank = jnp.zeros((nsel, TQ), F32)
    for j in range(nsel):
        vj = imp[j:j + 1, :]
        beats = (vj > imp) | ((vj == imp) & (jcol > j))
        rank = rank + beats.astype(F32)
    sel_ref[...] = jnp.where(rank < n_top, 0.0, NEG)

    bpt = TS // NSA_SEL

    def sel_scores(kt):
        s = _dot_nt(ksn_ref[kt], qrows)
        parts = []
        for jb in range(bpt):
            srow = sel_ref[pl.ds(kt * bpt + jb, 1), :]
            srow = jnp.concatenate([srow] * HPG, axis=1)
            parts.append(jnp.broadcast_to(srow, (NSA_SEL, rows)))
        return s + jnp.concatenate(parts, axis=0)

    def online(s, carry, vt):
        m, l, acc = carry
        m_new = jnp.maximum(m, jnp.max(s, axis=0, keepdims=True))
        alpha = jnp.exp(m - m_new)
        pexp = jnp.exp(s - m_new)
        l = alpha * l + jnp.sum(pexp, axis=0, keepdims=True)
        acc = alpha * acc + _dot(vt, pexp.astype(BF16))
        return m_new, l, acc

    def sel_body(kt, carry):
        return online(sel_scores(kt), carry, vst_ref[kt])

    init = (jnp.full((1, rows), NEG, F32), jnp.zeros((1, rows), F32), jnp.zeros((DK, rows), F32))
    kd = s0 // TS
    carry = lax.fori_loop(0, kd, sel_body, init)
    kpos = kd * TS + _iota2((TS, 1), 0)
    _, l_s, acc_s = online(jnp.where(kpos <= tq, sel_scores(kd), NEG), carry, vst_ref[kd])

    nwt = (NSA_WINDOW + TQ) // TW
    kt0 = i - NSA_WINDOW // TW
    idx = [jnp.maximum(kt0 + d, 0) for d in range(nwt)]
    sw = _dot_nt(jnp.concatenate([kwn_ref[ix] for ix in idx], axis=0), qrows)
    kposw = kt0 * TW + _iota2((nwt * TW, 1), 0)
    sw = jnp.where((kposw <= tq) & (kposw > tq - NSA_WINDOW) & (kposw >= 0), sw, NEG)
    pw = jnp.exp(sw - jnp.max(sw, axis=0, keepdims=True))
    l_w = jnp.sum(pw, axis=0, keepdims=True)
    pwb = pw.astype(BF16)
    acc_w = _dot(vwt_ref[idx[0]], pwb[0:TW])
    for d in range(1, nwt):
        acc_w = acc_w + _dot(vwt_ref[idx[d]], pwb[d * TW:(d + 1) * TW])

    gt = _sigmoid(gate_ref[0, 0])

    def grow(c):
        return jnp.concatenate([gt[c * HPG + r:c * HPG + r + 1, :] for r in range(HPG)], axis=1)

    out = (grow(0) * oc + grow(1) * (acc_s / jnp.maximum(l_s, 1e-30))
           + grow(2) * (acc_w / jnp.maximum(l_w, 1e-30)))
    for r in range(HPG):
        o_ref[:, r * DK:(r + 1) * DK] = out[:, r * TQ:(r + 1) * TQ].T.astype(o_ref.dtype)


def nsa_attention(proj, kvc, gate, q_g, k_g, bsz, seq):
    G, HPG, DK, TQ, TS, TW = NSA_G, NSA_HPG, NSA_DK, NSA_TQ, NSA_TKS, NSA_TKW
    assert TQ == TW and seq % TS == 0
    t = proj.shape[0]
    nq = seq // TQ
    ncmp = kvc.shape[2]
    nsel = seq // NSA_SEL
    n_top = min(NSA_TOP_N, nsel)
    qw = HPG * DK
    kv0 = G * HPG

    def kvspec(which):
        return pl.BlockSpec((seq, DK), lambda b, g, i: (b, kv0 + which * G + g))

    return pl.pallas_call(
        functools.partial(_nsa_attn_kernel, n_top=n_top),
        grid=(bsz, G, nq),
        in_specs=[
            pl.BlockSpec((TQ, qw), lambda b, g, i: (b * nq + i, g)),
            kvspec(2), kvspec(3), kvspec(4), kvspec(5),
            pl.BlockSpec((1, 1, ncmp, DK), lambda b, g, i: (0, b, 0, g)),
            pl.BlockSpec((1, 1, ncmp, DK), lambda b, g, i: (1, b, 0, g)),
            pl.BlockSpec((1, 1, 4 * HPG, TQ), lambda b, g, i: (b, g, 0, i)),
            pl.BlockSpec((1, DK), lambda b, g, i: (0, 0)),
            pl.BlockSpec((3, DK), lambda b, g, i: (0, 0)),
        ],
        out_specs=pl.BlockSpec((TQ, qw), lambda b, g, i: (b * nq + i, g)),
        out_shape=jax.ShapeDtypeStruct((t, G * qw), BF16),
        scratch_shapes=[
            pltpu.VMEM((seq // TS, TS, DK), BF16),
            pltpu.VMEM((seq // TS, DK, TS), BF16),
            pltpu.VMEM((seq // TW, TW, DK), BF16),
            pltpu.VMEM((seq // TW, DK, TW), BF16),
            pltpu.VMEM((ncmp, DK), BF16),
            pltpu.VMEM((DK, ncmp), BF16),
            pltpu.VMEM((nsel, TQ), F32),
        ],
        compiler_params=_cparams(("parallel", "parallel", "arbitrary")),
        name="nsa_attention",
    )(proj, proj, proj, proj, proj, kvc, kvc, gate, q_g.reshape(1, DK), k_g)


def nsa_layer(h, g_mix, w_in, q_g, k_g, cmp_pos, cmp_w1, cmp_w2, w_out, bsz, seq):
    G, HPG, DK = NSA_G, NSA_HPG, NSA_DK
    H = G * HPG
    nmain = H * DK + 6 * G * DK
    proj = norm_matmul(h, g_mix, w_in[:, :nmain].astype(BF16))
    gate_raw = norm_matmul(h, g_mix, _pad_cols(w_in[:, nmain:], LANES).astype(BF16))[:, :3 * H]
    ngrp = seq // NSA_STRIDE
    kvw = G * DK
    xkv = jnp.stack([
        proj[:, H * DK:H * DK + kvw].reshape(bsz, ngrp, NSA_STRIDE * kvw),
        proj[:, H * DK + kvw:H * DK + 2 * kvw].reshape(bsz, ngrp, NSA_STRIDE * kvw),
    ])
    kvc = nsa_compress(xkv, cmp_pos, cmp_w1, cmp_w2, k_g[0])
    gate = gate_raw.reshape(bsz, seq, G, HPG, 3).transpose(0, 2, 4, 3, 1).reshape(bsz, G, 3 * HPG, seq)
    gate = jnp.pad(gate, ((0, 0), (0, 0), (0, HPG), (0, 0)))
    o = nsa_attention(proj, kvc, gate, q_g, k_g, bsz, seq)
    return matmul_residual(o, w_out.astype(BF16), h)


def kernel(x, p, norm_mix_g, norm_mlp_g, mlp_w1, mlp_w2, ple_norm_g, ple_w_gate, ple_w_proj, nsa_w_in, nsa_q_g, nsa_k_g, nsa_cmp_pos, nsa_cmp_w1, nsa_cmp_w2, nsa_w_out, m2_w_in, m2_conv_w, m2_conv_b, m2_dt_bias, m2_a_log, m2_d, m2_norm_g, m2_w_out, gdn_w_in, gdn_conv_w, gdn_a_log, gdn_dt_bias, gdn_norm_g, gdn_w_out, ml_w_in, ml_gate_b, ml_norm_g, ml_w_out):
    bsz, seq, d = x.shape
    h = x.reshape(bsz * seq, d)
    for i in range(p.shape[0]):
        kind, j = i % 4, i // 4
        if kind == 0:
            h = nsa_layer(h, norm_mix_g[i], nsa_w_in[j], nsa_q_g[j], nsa_k_g[j], nsa_cmp_pos[j], nsa_cmp_w1[j], nsa_cmp_w2[j], nsa_w_out[j], bsz, seq)
        elif kind == 1:
            h = mamba2_layer(h, norm_mix_g[i], m2_w_in[j], m2_conv_w[j], m2_conv_b[j], m2_dt_bias[j], m2_a_log[j], m2_d[j], m2_norm_g[j], m2_w_out[j], bsz, seq)
        elif kind == 2:
            h = gdn_layer(h, norm_mix_g[i], gdn_w_in[j], gdn_conv_w[j], gdn_a_log[j], gdn_dt_bias[j], gdn_norm_g[j], gdn_w_out[j], bsz, seq)
        elif kind == 3:
            h = mlstm_layer(h, norm_mix_g[i], ml_w_in[j], ml_gate_b[j], ml_norm_g[j], ml_w_out[j], bsz, seq)
        h = mlp_block(h, norm_mlp_g[i], mlp_w1[i].astype(BF16), mlp_w2[i].astype(BF16))
        h = ple_update(h, ple_norm_g[i], ple_w_gate[i].astype(BF16), p[i].reshape(bsz * seq, -1), ple_w_proj[i].astype(BF16))
    return h.reshape(bsz, seq, d)
```

```python
import functools
import math

import jax
import jax.numpy as jnp
from jax import lax
from jax.experimental import pallas as pl
from jax.experimental.pallas import tpu as pltpu

F32 = jnp.float32
BF16 = jnp.bfloat16
EPS = 1e-6
NEG = -1e30
HIGHEST = lax.Precision.HIGHEST

VMEM_LIMIT_BYTES = 56 * 1024 * 1024
LANES = 128


def _cparams(sem):
    return pltpu.CompilerParams(dimension_semantics=sem, vmem_limit_bytes=VMEM_LIMIT_BYTES)


def _silu(x):
    return x * (1.0 / (1.0 + jnp.exp(-x)))


def _sigmoid(x):
    return 1.0 / (1.0 + jnp.exp(-x))


def _softplus(x):
    return jnp.maximum(x, 0.0) + jnp.log(1.0 + jnp.exp(-jnp.abs(x)))


def _dot(a, b, **kw):
    return jnp.dot(a, b, preferred_element_type=F32, **kw)


def _dot_nt(a, b, **kw):
    return lax.dot_general(a, b, (((1,), (1,)), ((), ())), preferred_element_type=F32, **kw)


def _dot_tn(a, b, **kw):
    return lax.dot_general(a, b, (((0,), (0,)), ((), ())), preferred_element_type=F32, **kw)


def _norm_mm_kernel(x_ref, g_ref, w_ref, o_ref, xn_ref):
    @pl.when(pl.program_id(1) == 0)
    def _():
        x = x_ref[...]
        ms = jnp.mean(x * x, axis=-1, keepdims=True)
        xn_ref[...] = (x * lax.rsqrt(ms + EPS) * g_ref[...]).astype(BF16)

    o_ref[...] = _dot(xn_ref[...], w_ref[...]).astype(o_ref.dtype)


def _tile(dim, want):
    t = min(want, dim)
    while dim % t:
        t //= 2
    return t


def norm_matmul(x, g, w, *, out_dtype=F32, tm=1024, tn=1024):
    m, k = x.shape
    n = w.shape[1]
    tm, tn = _tile(m, tm), _tile(n, tn)
    return pl.pallas_call(
        _norm_mm_kernel,
        grid=(m // tm, n // tn),
        in_specs=[
            pl.BlockSpec((tm, k), lambda i, j: (i, 0)),
            pl.BlockSpec((1, k), lambda i, j: (0, 0)),
            pl.BlockSpec((k, tn), lambda i, j: (0, j)),
        ],
        out_specs=pl.BlockSpec((tm, tn), lambda i, j: (i, j)),
        out_shape=jax.ShapeDtypeStruct((m, n), out_dtype),
        scratch_shapes=[pltpu.VMEM((tm, k), BF16)],
        compiler_params=_cparams(("parallel", "arbitrary")),
        name="norm_matmul",
    )(x, g.reshape(1, k), w)


def _mm_res_kernel(x_ref, w_ref, r_ref, o_ref):
    o_ref[...] = r_ref[...] + _dot(x_ref[...], w_ref[...])


def matmul_residual(x, w, res, *, tm=1024, tn=512):
    m, k = x.shape
    n = w.shape[1]
    tm, tn = _tile(m, tm), _tile(n, tn)
    return pl.pallas_call(
        _mm_res_kernel,
        grid=(m // tm, n // tn),
        in_specs=[
            pl.BlockSpec((tm, k), lambda i, j: (i, 0)),
            pl.BlockSpec((k, tn), lambda i, j: (0, j)),
            pl.BlockSpec((tm, tn), lambda i, j: (i, j)),
        ],
        out_specs=pl.BlockSpec((tm, tn), lambda i, j: (i, j)),
        out_shape=jax.ShapeDtypeStruct((m, n), F32),
        compiler_params=_cparams(("parallel", "arbitrary")),
        name="matmul_residual",
    )(x, w, res)


def _ple_kernel(x_ref, g_ref, wg_ref, p_ref, wp_ref, r_ref, o_ref, xn_ref):
    @pl.when(pl.program_id(1) == 0)
    def _():
        x = x_ref[...]
        ms = jnp.mean(x * x, axis=-1, keepdims=True)
        xn_ref[...] = (x * lax.rsqrt(ms + EPS) * g_ref[...]).astype(BF16)

    gate = _sigmoid(_dot(xn_ref[...], wg_ref[...]))
    proj = _dot(p_ref[...].astype(BF16), wp_ref[...])
    o_ref[...] = r_ref[...] + gate * proj


def ple_update(h, g, wg, p, wp, *, tm=1024, tn=512):
    m, k = h.shape
    n = wg.shape[1]
    kp = p.shape[1]
    tm, tn = _tile(m, tm), _tile(n, tn)
    return pl.pallas_call(
        _ple_kernel,
        grid=(m // tm, n // tn),
        in_specs=[
            pl.BlockSpec((tm, k), lambda i, j: (i, 0)),
            pl.BlockSpec((1, k), lambda i, j: (0, 0)),
            pl.BlockSpec((k, tn), lambda i, j: (0, j)),
            pl.BlockSpec((tm, kp), lambda i, j: (i, 0)),
            pl.BlockSpec((kp, tn), lambda i, j: (0, j)),
            pl.BlockSpec((tm, tn), lambda i, j: (i, j)),
        ],
        out_specs=pl.BlockSpec((tm, tn), lambda i, j: (i, j)),
        out_shape=jax.ShapeDtypeStruct((m, n), F32),
        scratch_shapes=[pltpu.VMEM((tm, k), BF16)],
        compiler_params=_cparams(("parallel", "arbitrary")),
        name="ple_update",
    )(h, g.reshape(1, k), wg, p, wp, h)


def _mlp_kernel(x_ref, g_ref, w1_ref, w2_ref, o_ref, xn_ref):
    @pl.when(pl.program_id(1) == 0)
    def _():
        x = x_ref[...]
        ms = jnp.mean(x * x, axis=-1, keepdims=True)
        xn_ref[...] = (x * lax.rsqrt(ms + EPS) * g_ref[...]).astype(BF16)
        o_ref[...] = x

    a = jnp.maximum(_dot(xn_ref[...], w1_ref[...]), 0.0)
    o_ref[...] += _dot((a * a).astype(BF16), w2_ref[...])


def mlp_block(h, g, w1, w2, *, tm=1024, tf=512):
    m, k = h.shape
    f = w1.shape[1]
    tm, tf = _tile(m, tm), _tile(f, tf)
    return pl.pallas_call(
        _mlp_kernel,
        grid=(m // tm, f // tf),
        in_specs=[
            pl.BlockSpec((tm, k), lambda i, j: (i, 0)),
            pl.BlockSpec((1, k), lambda i, j: (0, 0)),
            pl.BlockSpec((k, tf), lambda i, j: (0, j)),
            pl.BlockSpec((tf, k), lambda i, j: (j, 0)),
        ],
        out_specs=pl.BlockSpec((tm, k), lambda i, j: (i, 0)),
        out_shape=jax.ShapeDtypeStruct((m, k), F32),
        scratch_shapes=[pltpu.VMEM((tm, k), BF16)],
        compiler_params=_cparams(("parallel", "arbitrary")),
        name="mlp_block",
    )(h, g.reshape(1, k), w1, w2)


def _pad_cols(w, n):
    return jnp.pad(w, ((0, 0), (0, n - w.shape[1])))


def _iota2(shape, dim):
    return lax.broadcasted_iota(jnp.int32, shape, dim)


def _col_to_row(col, eye):
    return jnp.sum(jnp.where(eye, col, 0.0), axis=0, keepdims=True)


def _log_sigmoid(x):
    return jnp.minimum(x, 0.0) - jnp.log(1.0 + jnp.exp(-jnp.abs(x)))


ML_HEADS = 4
ML_CHUNK = 64


def _mlstm_kernel(q_ref, k_ref, v_ref, og_ref, gt_ref, gb_ref, ng_ref, o_ref, c_ref, n_ref, m_ref, *, dq, dv):
    L = ML_CHUNK

    @pl.when(pl.program_id(1) == 0)
    def _():
        c_ref[...] = jnp.zeros_like(c_ref)
        n_ref[...] = jnp.zeros_like(n_ref)
        m_ref[...] = jnp.zeros_like(m_ref)

    row = _iota2((L, L), 0)
    col = _iota2((L, L), 1)
    incl = row >= col
    eye = row == col
    gates = gt_ref[...] + gb_ref[...]
    scale = dq ** -0.5
    for h in range(ML_HEADS):
        q = q_ref[:, h * dq:(h + 1) * dq] * scale
        k = k_ref[:, h * dq:(h + 1) * dq]
        v = v_ref[:, h * dv:(h + 1) * dv]
        li_c = gates[:, h:h + 1]
        lf_c = _log_sigmoid(gates[:, ML_HEADS + h:ML_HEADS + h + 1])
        li_r = _col_to_row(li_c, eye)
        lf_r = _col_to_row(lf_c, eye)
        b_c = jnp.sum(jnp.where(incl, lf_r, 0.0), axis=1, keepdims=True)
        b_r = _col_to_row(b_c, eye)
        m_prev = m_ref[h]
        dlog = jnp.where(incl, b_c - b_r + li_r, NEG)
        inter = b_c + m_prev
        m_t = jnp.maximum(inter, jnp.max(dlog, axis=1, keepdims=True))
        w_inter = jnp.exp(inter - m_t)
        qb = q.astype(BF16)
        kb = k.astype(BF16)
        vb = v.astype(BF16)
        s = _dot_nt(qb, kb) * jnp.where(incl, jnp.exp(dlog - m_t), 0.0)
        c_prev = c_ref[h]
        num = _dot(s.astype(BF16), vb) + w_inter * _dot(qb, c_prev.astype(BF16))
        qn = jnp.sum(q * n_ref[h], axis=1, keepdims=True)
        den = jnp.sum(s, axis=1, keepdims=True) + w_inter * qn
        hc = num / jnp.maximum(jnp.abs(den), jnp.exp(-m_t))
        tot = b_c[L - 1:L, :]
        ulog_c = tot - b_c + li_c
        m_new = jnp.maximum(tot + m_prev, jnp.max(ulog_c, axis=0, keepdims=True))
        wu_c = jnp.exp(ulog_c - m_new)
        dec = jnp.exp(tot + m_prev - m_new)
        kw = k * wu_c
        c_ref[h] = dec * c_prev + _dot_tn(kw.astype(BF16), vb)
        n_ref[h] = dec * n_ref[h] + jnp.sum(kw, axis=0, keepdims=True)
        m_ref[h] = m_new
        ms = jnp.mean(hc * hc, axis=1, keepdims=True)
        hn = hc * lax.rsqrt(ms + EPS) * ng_ref[:, h * dv:(h + 1) * dv]
        o_ref[:, h * dv:(h + 1) * dv] = (_sigmoid(og_ref[:, h * dv:(h + 1) * dv]) * hn).astype(o_ref.dtype)


def mlstm_core(proj, gates, gate_b, norm_g, bsz, seq):
    H, L = ML_HEADS, ML_CHUNK
    t = proj.shape[0]
    dq = proj.shape[1] // (6 * H)
    dv = 2 * dq
    nc = seq // L
    gb = jnp.pad(gate_b, (0, LANES - gate_b.shape[0])).reshape(1, LANES)
    rowmap = lambda b, c: b * nc + c
    return pl.pallas_call(
        functools.partial(_mlstm_kernel, dq=dq, dv=dv),
        grid=(bsz, nc),
        in_specs=[
            pl.BlockSpec((L, H * dq), lambda b, c: (rowmap(b, c), 0)),
            pl.BlockSpec((L, H * dq), lambda b, c: (rowmap(b, c), 1)),
            pl.BlockSpec((L, H * dv), lambda b, c: (rowmap(b, c), 1)),
            pl.BlockSpec((L, H * dv), lambda b, c: (rowmap(b, c), 2)),
            pl.BlockSpec((L, LANES), lambda b, c: (rowmap(b, c), 0)),
            pl.BlockSpec((1, LANES), lambda b, c: (0, 0)),
            pl.BlockSpec((1, H * dv), lambda b, c: (0, 0)),
        ],
        out_specs=pl.BlockSpec((L, H * dv), lambda b, c: (rowmap(b, c), 0)),
        out_shape=jax.ShapeDtypeStruct((t, H * dv), BF16),
        scratch_shapes=[
            pltpu.VMEM((H, dq, dv), F32),
            pltpu.VMEM((H, 1, dq), F32),
            pltpu.VMEM((H, 1, 1), F32),
        ],
        compiler_params=_cparams(("parallel", "arbitrary")),
        name="mlstm_core",
    )(proj, proj, proj, proj, gates, gb, norm_g.reshape(1, H * dv))


def mlstm_layer(h, g_mix, w_in, gate_b, norm_g, w_out, bsz, seq):
    nmain = w_in.shape[1] - 2 * ML_HEADS
    proj = norm_matmul(h, g_mix, w_in[:, :nmain].astype(BF16))
    gates = norm_matmul(h, g_mix, _pad_cols(w_in[:, nmain:], LANES).astype(BF16))
    y = mlstm_core(proj, gates, gate_b, norm_g, bsz, seq)
    return matmul_residual(y, w_out.astype(BF16), h)


M2_CHUNK = 128
M2_GROUPS = 8
M2_STATE = 128
M2_HEAD_DIM = 64
CONV_K = 4
CARRY_ROWS = 8


def _causal_conv_silu(x_raw, carry_ref, w, b):
    n = x_raw.shape[0]
    xe = jnp.concatenate([carry_ref[...], x_raw], axis=0)
    acc = x_raw * w[CONV_K - 1:CONV_K, :]
    if b is not None:
        acc = acc + b
    for k in range(CONV_K - 1):
        sh = CONV_K - 1 - k
        acc = acc + pltpu.roll(xe, sh, axis=0)[CARRY_ROWS:, :] * w[k:k + 1, :]
    carry_ref[...] = x_raw[n - CARRY_ROWS:, :]
    return _silu(acc)


def _ssd_kernel(z_ref, xs_ref, bm_ref, cm_ref, dtc_ref, dtr_ref, cwx_ref, cwb_ref, cwc_ref, cbx_ref, cbb_ref, cbc_ref,
                dtbc_ref, dtbr_ref, alogc_ref, alogr_ref, dsk_ref, ng_ref, o_ref, st_ref, cx_ref, cbm_ref, ccm_ref):
    L, P = M2_CHUNK, M2_HEAD_DIM
    PW = 2 * P
    npair = st_ref.shape[1] // PW
    assert L == PW == st_ref.shape[0]

    @pl.when(pl.program_id(2) == 0)
    def _():
        st_ref[...] = jnp.zeros_like(st_ref)
        cx_ref[...] = jnp.zeros_like(cx_ref)
        cbm_ref[...] = jnp.zeros_like(cbm_ref)
        ccm_ref[...] = jnp.zeros_like(ccm_ref)

    xs = _causal_conv_silu(xs_ref[...], cx_ref, cwx_ref[...], cbx_ref[...])
    bm = _causal_conv_silu(bm_ref[...], cbm_ref, cwb_ref[...], cbb_ref[...])
    cm = _causal_conv_silu(cm_ref[...], ccm_ref, cwc_ref[...], cbc_ref[...])
    dt_c = _softplus(dtc_ref[0, 0] + dtbc_ref[0])
    dt_r = _softplus(dtr_ref[0, 0] + dtbr_ref[0])
    lac_c = dt_c * (-jnp.exp(alogc_ref[0]))
    lac_r = dt_r * (-jnp.exp(alogr_ref[0]))
    row = _iota2((L, L), 0)
    col = _iota2((L, L), 1)
    tri = row >= col
    cum_c = _dot(tri.astype(F32), lac_c, precision=HIGHEST)
    cum_r = _dot(lac_r, (row <= col).astype(F32), precision=HIGHEST)
    cl_c = cum_c[L - 1:L, :]
    cl_r = cum_r[:, L - 1:L]
    cb = _dot_nt(cm.astype(BF16), bm.astype(BF16))
    bmt = bm.T
    st_all = st_ref[...]
    yi = _dot(cm.astype(BF16), st_all.astype(BF16))
    first = _iota2((1, PW), 1) < P

    prs = range(npair)
    colb = [[jnp.broadcast_to(cum_c[:, h:h + 1], (L, L)) for h in (2 * j, 2 * j + 1)] for j in prs]
    rowv = [[cum_r[h:h + 1, :] for h in (2 * j, 2 * j + 1)] for j in prs]
    dtv = [[dt_r[h:h + 1, :] for h in (2 * j, 2 * j + 1)] for j in prs]
    dec = [[jnp.exp(jnp.where(tri, colb[j][i] - rowv[j][i], NEG)) for i in range(2)] for j in prs]
    mm = [[cb * dec[j][i] * dtv[j][i] for i in range(2)] for j in prs]
    ww = [[bmt * (dtv[j][i] * jnp.exp(cl_r[2 * j + i:2 * j + i + 1, :] - rowv[j][i])) for i in range(2)] for j in prs]
    lhs = [jnp.concatenate([jnp.concatenate(mm[j], axis=1), jnp.concatenate(ww[j], axis=1)], axis=0).astype(BF16)
           for j in prs]
    xp = [xs[:, j * PW:(j + 1) * PW] for j in prs]
    rhs = [jnp.concatenate([jnp.where(first, xp[j], 0.0), jnp.where(first, 0.0, xp[j])], axis=0).astype(BF16)
           for j in prs]
    out = [_dot(lhs[j], rhs[j]) for j in prs]
    esc = [jnp.where(first, jnp.exp(colb[j][0]), jnp.exp(colb[j][1])) for j in prs]
    sdec = [jnp.where(first, jnp.exp(cl_c[:, 2 * j:2 * j + 1]), jnp.exp(cl_c[:, 2 * j + 1:2 * j + 2])) for j in prs]
    y = jnp.concatenate([out[j][:L] + yi[:, j * PW:(j + 1) * PW] * esc[j] for j in prs], axis=1)
    st_ref[...] = jnp.concatenate([st_all[:, j * PW:(j + 1) * PW] * sdec[j] + out[j][L:] for j in prs], axis=1)
    y = y + dsk_ref[...] * xs
    y = y * _silu(z_ref[...])
    ms = jnp.mean(y * y, axis=1, keepdims=True)
    o_ref[...] = (y * lax.rsqrt(ms + EPS) * ng_ref[...]).astype(o_ref.dtype)


def ssd_core(proj, dt_raw, conv_w, conv_b, dt_bias, a_log, d_skip, norm_g, bsz, seq):
    L, G, N, P = M2_CHUNK, M2_GROUPS, M2_STATE, M2_HEAD_DIM
    t = proj.shape[0]
    heads = dt_raw.shape[1]
    R = heads // G
    di = heads * P
    gw = di // G
    nc = seq // L
    dtc = dt_raw.reshape(bsz, seq, G, R).transpose(0, 2, 1, 3)
    dtr = dt_raw.reshape(bsz, seq, G, R).transpose(0, 2, 3, 1)
    cw = conv_w
    cbias = conv_b.reshape(1, -1)
    row = lambda b, g, c: b * nc + c
    nzx = di // gw
    nb0 = 2 * di // N
    cwb0 = di // N
    return pl.pallas_call(
        _ssd_kernel,
        grid=(bsz, G, nc),
        in_specs=[
            pl.BlockSpec((L, gw), lambda b, g, c: (row(b, g, c), g)),
            pl.BlockSpec((L, gw), lambda b, g, c: (row(b, g, c), nzx + g)),
            pl.BlockSpec((L, N), lambda b, g, c: (row(b, g, c), nb0 + g)),
            pl.BlockSpec((L, N), lambda b, g, c: (row(b, g, c), nb0 + G + g)),
            pl.BlockSpec((1, 1, L, R), lambda b, g, c: (b, g, c, 0)),
            pl.BlockSpec((1, 1, R, L), lambda b, g, c: (b, g, 0, c)),
            pl.BlockSpec((CONV_K, gw), lambda b, g, c: (0, g)),
            pl.BlockSpec((CONV_K, N), lambda b, g, c: (0, cwb0 + g)),
            pl.BlockSpec((CONV_K, N), lambda b, g, c: (0, cwb0 + G + g)),
            pl.BlockSpec((1, gw), lambda b, g, c: (0, g)),
            pl.BlockSpec((1, N), lambda b, g, c: (0, cwb0 + g)),
            pl.BlockSpec((1, N), lambda b, g, c: (0, cwb0 + G + g)),
            pl.BlockSpec((1, 1, R), lambda b, g, c: (g, 0, 0)),
            pl.BlockSpec((1, R, 1), lambda b, g, c: (g, 0, 0)),
            pl.BlockSpec((1, 1, R), lambda b, g, c: (g, 0, 0)),
            pl.BlockSpec((1, R, 1), lambda b, g, c: (g, 0, 0)),
            pl.BlockSpec((1, gw), lambda b, g, c: (0, g)),
            pl.BlockSpec((1, gw), lambda b, g, c: (0, g)),
        ],
        out_specs=pl.BlockSpec((L, gw), lambda b, g, c: (row(b, g, c), g)),
        out_shape=jax.ShapeDtypeStruct((t, di), BF16),
        scratch_shapes=[
            pltpu.VMEM((N, gw), F32),
            pltpu.VMEM((CARRY_ROWS, gw), F32),
            pltpu.VMEM((CARRY_ROWS, N), F32),
            pltpu.VMEM((CARRY_ROWS, N), F32),
        ],
        compiler_params=_cparams(("parallel", "parallel", "arbitrary")),
        name="ssd_core",
    )(proj, proj, proj, proj, dtc, dtr, cw, cw, cw, cbias, cbias, cbias,
      dt_bias.reshape(G, 1, R), dt_bias.reshape(G, R, 1), a_log.reshape(G, 1, R), a_log.reshape(G, R, 1),
      jnp.repeat(d_skip, P).reshape(1, di), norm_g.reshape(1, di))


def mamba2_layer(h, g_mix, w_in, conv_w, conv_b, dt_bias, a_log, d_skip, norm_g, w_out, bsz, seq):
    heads = dt_bias.shape[0]
    nmain = w_in.shape[1] - heads
    proj = norm_matmul(h, g_mix, w_in[:, :nmain].astype(BF16))
    dt_raw = norm_matmul(h, g_mix, _pad_cols(w_in[:, nmain:], LANES).astype(BF16))[:, :heads]
    y = ssd_core(proj, dt_raw, conv_w, conv_b, dt_bias, a_log, d_skip, norm_g, bsz, seq)
    return matmul_residual(y, w_out.astype(BF16), h)


GDN_CHUNK = 64
GDN_DH = 128
GDN_STACK = 4
GDN_STACKS = 4


def _l2norm(x):
    return x * lax.rsqrt(jnp.sum(x * x, axis=1, keepdims=True) + EPS)


def _gdn_kernel(q_ref, k_ref, v_ref, z_ref, ba_ref, cwq_ref, cwk_ref, cwv_ref, alog_ref, dtb_ref, ng_ref,
                o_ref, st_ref, cq_ref, ck_ref, cv_ref):
    L, DH, HS = GDN_CHUNK, GDN_DH, GDN_STACK
    nv = GDN_STACK * GDN_STACKS
    SL = HS * L

    @pl.when(pl.program_id(2) == 0)
    def _():
        st_ref[...] = jnp.zeros_like(st_ref)
        cq_ref[...] = jnp.zeros_like(cq_ref)
        ck_ref[...] = jnp.zeros_like(ck_ref)
        cv_ref[...] = jnp.zeros_like(cv_ref)

    q = _causal_conv_silu(q_ref[...], cq_ref, cwq_ref[...], None)
    k = _causal_conv_silu(k_ref[...], ck_ref, cwk_ref[...], None)
    v = _causal_conv_silu(v_ref[...], cv_ref, cwv_ref[...], None)
    ba = ba_ref[0, 0]
    beta = _sigmoid(ba[:, :nv])
    gg = -jnp.exp(alog_ref[0]) * _softplus(ba[:, nv:2 * nv] + dtb_ref[0])
    qn = [_l2norm(q[:, i * DH:(i + 1) * DH]) * DH ** -0.5 for i in range(nv // 2)]
    kn = [_l2norm(k[:, i * DH:(i + 1) * DH]) for i in range(nv // 2)]

    ri = _iota2((SL, SL), 0)
    ci = _iota2((SL, SL), 1)
    same = (ri // L) == (ci // L)
    incl = same & (ri >= ci)
    strict = same & (ri > ci)
    eye = ri == ci
    last = same & ((ci % L) == L - 1)

    stacks = range(GDN_STACKS)
    heads = range(HS)
    rows = [slice(i * L, (i + 1) * L) for i in heads]
    hv0 = [s * HS for s in stacks]
    kst = [jnp.concatenate([kn[(hv0[s] + i) // 2] for i in heads], axis=0) for s in stacks]
    qst = [jnp.concatenate([qn[(hv0[s] + i) // 2] for i in heads], axis=0) for s in stacks]
    vst = [jnp.concatenate([v[:, (hv0[s] + i) * DH:(hv0[s] + i + 1) * DH] for i in heads], axis=0) for s in stacks]
    beta_c = [jnp.concatenate([beta[:, hv0[s] + i:hv0[s] + i + 1] for i in heads], axis=0) for s in stacks]
    g_c = [jnp.concatenate([gg[:, hv0[s] + i:hv0[s] + i + 1] for i in heads], axis=0) for s in stacks]
    g_r = [_col_to_row(g_c[s], eye) for s in stacks]
    cum_c = [jnp.sum(jnp.where(incl, g_r[s], 0.0), axis=1, keepdims=True) for s in stacks]
    cum_r = [_col_to_row(cum_c[s], eye) for s in stacks]
    dmask = [jnp.exp(jnp.where(incl, cum_c[s] - cum_r[s], NEG)) for s in stacks]
    kb = [kst[s] * beta_c[s] for s in stacks]
    kst_b = [kst[s].astype(BF16) for s in stacks]
    nmat = [jnp.where(strict, _dot_nt(kb[s].astype(BF16), kst_b[s]) * dmask[s], 0.0) for s in stacks]
    ecum = [jnp.exp(cum_c[s]) for s in stacks]
    rhs = [jnp.concatenate([vst[s] * beta_c[s], kb[s] * ecum[s]], axis=1) for s in stacks]
    pw_b = [nmat[s].astype(BF16) for s in stacks]
    sol = [rhs[s] - _dot(pw_b[s], rhs[s].astype(BF16)) for s in stacks]
    for _ in range(5):
        pw_b = [_dot(pw_b[s], pw_b[s]).astype(BF16) for s in stacks]
        sol = [sol[s] + _dot(pw_b[s], sol[s].astype(BF16)) for s in stacks]
    qe = [qst[s] * ecum[s] for s in stacks]
    attn = [_dot_nt(qst[s].astype(BF16), kst_b[s]) * dmask[s] for s in stacks]
    cl_c = [jnp.sum(jnp.where(last, cum_r[s], 0.0), axis=1, keepdims=True) for s in stacks]
    kd = [kst[s] * jnp.exp(cl_c[s] - cum_c[s]) for s in stacks]
    st_all = [st_ref[s] for s in stacks]
    both = [[_dot(jnp.concatenate([sol[s][rows[i], DH:], qe[s][rows[i]]], axis=0).astype(BF16),
                  st_all[s][:, i * DH:(i + 1) * DH].astype(BF16)) for i in heads] for s in stacks]
    vnew = [jnp.concatenate([sol[s][rows[i], :DH] - both[s][i][:L] for i in heads], axis=0) for s in stacks]
    o = [jnp.concatenate([both[s][i][L:] for i in heads], axis=0)
         + _dot(attn[s].astype(BF16), vnew[s].astype(BF16)) for s in stacks]
    head_of_row = _iota2((SL, 1), 0) // L
    vbd = [jnp.concatenate([jnp.where(head_of_row == i, vnew[s], 0.0) for i in heads], axis=1) for s in stacks]
    decay = [jnp.concatenate([jnp.broadcast_to(jnp.exp(cl_c[s][i * L:i * L + 1, :]), (1, DH)) for i in heads], axis=1)
             for s in stacks]
    for s in stacks:
        st_ref[s] = st_all[s] * decay[s] + _dot(kd[s].T.astype(BF16), vbd[s].astype(BF16))
    for s in stacks:
        for i in heads:
            hv = hv0[s] + i
            oi = o[s][rows[i]]
            ms = jnp.mean(oi * oi, axis=1, keepdims=True)
            on = oi * lax.rsqrt(ms + EPS) * ng_ref[...]
            o_ref[:, hv * DH:(hv + 1) * DH] = (on * _silu(z_ref[:, hv * DH:(hv + 1) * DH])).astype(o_ref.dtype)


def gdn_core(proj, ba_raw, conv_w, a_log, dt_bias, norm_g, bsz, seq):
    L, DH = GDN_CHUNK, GDN_DH
    nv = GDN_STACK * GDN_STACKS
    t = proj.shape[0]
    hv_total = ba_raw.shape[1] // 2
    ng = hv_total // nv
    qw = (nv // 2) * DH
    vw = nv * DH
    nc = seq // L
    b_part = ba_raw[:, :hv_total].reshape(bsz, seq, ng, nv)
    a_part = ba_raw[:, hv_total:].reshape(bsz, seq, ng, nv)
    ba = jnp.concatenate([b_part, a_part], axis=-1).transpose(0, 2, 1, 3)
    row = lambda b, g, c: b * nc + c
    kq0 = ng
    v0 = 2 * ng * qw // vw
    return pl.pallas_call(
        _gdn_kernel,
        grid=(bsz, ng, nc),
        in_specs=[
            pl.BlockSpec((L, qw), lambda b, g, c: (row(b, g, c), g)),
            pl.BlockSpec((L, qw), lambda b, g, c: (row(b, g, c), kq0 + g)),
            pl.BlockSpec((L, vw), lambda b, g, c: (row(b, g, c), v0 + g)),
            pl.BlockSpec((L, vw), lambda b, g, c: (row(b, g, c), v0 + ng + g)),
            pl.BlockSpec((1, 1, L, 2 * nv), lambda b, g, c: (b, g, c, 0)),
            pl.BlockSpec((CONV_K, qw), lambda b, g, c: (0, g)),
            pl.BlockSpec((CONV_K, qw), lambda b, g, c: (0, kq0 + g)),
            pl.BlockSpec((CONV_K, vw), lambda b, g, c: (0, v0 + g)),
            pl.BlockSpec((1, 1, nv), lambda b, g, c: (g, 0, 0)),
            pl.BlockSpec((1, 1, nv), lambda b, g, c: (g, 0, 0)),
            pl.BlockSpec((1, DH), lambda b, g, c: (0, 0)),
        ],
        out_specs=pl.BlockSpec((L, vw), lambda b, g, c: (row(b, g, c), g)),
        out_shape=jax.ShapeDtypeStruct((t, hv_total * DH), BF16),
        scratch_shapes=[
            pltpu.VMEM((GDN_STACKS, DH, GDN_STACK * DH), F32),
            pltpu.VMEM((CARRY_ROWS, qw), F32),
            pltpu.VMEM((CARRY_ROWS, qw), F32),
            pltpu.VMEM((CARRY_ROWS, vw), F32),
        ],
        compiler_params=_cparams(("parallel", "parallel", "arbitrary")),
        name="gdn_core",
    )(proj, proj, proj, proj, ba, conv_w, conv_w, conv_w,
      a_log.reshape(ng, 1, nv), dt_bias.reshape(ng, 1, nv), norm_g.reshape(1, DH))


def gdn_layer(h, g_mix, w_in, conv_w, a_log, dt_bias, norm_g, w_out, bsz, seq):
    hv = a_log.shape[0]
    nmain = w_in.shape[1] - 2 * hv
    proj = norm_matmul(h, g_mix, w_in[:, :nmain].astype(BF16))
    ba_raw = norm_matmul(h, g_mix, _pad_cols(w_in[:, nmain:], LANES).astype(BF16))[:, :2 * hv]
    y = gdn_core(proj, ba_raw, conv_w, a_log, dt_bias, norm_g, bsz, seq)
    return matmul_residual(y, w_out.astype(BF16), h)


NSA_DK = 128
NSA_G = 4
NSA_HPG = 4
NSA_STRIDE = 16
NSA_CMP_BLOCK = 32
NSA_SEL = 64
NSA_TOP_N = 16
NSA_WINDOW = 512
NSA_TQ = 128
NSA_TKS = 512
NSA_TKW = 128
FORCE_BONUS = 1e4


def _cmp_kernel(x_ref, pa_ref, pb_ref, wa_ref, wb_ref, w2_ref, kg_ref, o_ref, acca_ref, accb_ref):
    kk = pl.program_id(2)

    @pl.when(kk == 0)
    def _():
        acca_ref[...] = jnp.zeros_like(acca_ref)
        accb_ref[...] = jnp.zeros_like(accb_ref)

    x = x_ref[0, 0]
    acca_ref[...] += _dot((x + pa_ref[0]).astype(BF16), wa_ref[0])
    accb_ref[...] += _dot((x + pb_ref[0]).astype(BF16), wb_ref[0])

    @pl.when(kk == pl.num_programs(2) - 1)
    def _():
        nrow = acca_ref.shape[0]
        pre = acca_ref[...] + pltpu.roll(accb_ref[...], nrow - 1, axis=0)
        y = _dot(_silu(pre).astype(BF16), w2_ref[0])
        is_k = pl.program_id(0) == 0
        for g in range(NSA_G):
            yg = y[:, g * NSA_DK:(g + 1) * NSA_DK]
            ms = jnp.mean(yg * yg, axis=1, keepdims=True)
            yn = yg * lax.rsqrt(ms + EPS) * kg_ref[...]
            o_ref[0, 0, :, g * NSA_DK:(g + 1) * NSA_DK] = jnp.where(is_k, yn, yg)


def nsa_compress(xkv, pos, w1, w2, kg0, *, tk=2048):
    _, bsz, ngrp, kdim = xkv.shape
    G, dk, half = NSA_G, NSA_DK, NSA_CMP_BLOCK // 2
    eye = jnp.eye(G, dtype=w1.dtype)

    def expand_w1(w):
        return jnp.einsum("clde,gh->clgdhe", w, eye).reshape(2, half * G * dk, G * dk).astype(BF16)

    def expand_pos(ps):
        return jnp.broadcast_to(ps[:, :, None, :], (2, half, G, dk)).reshape(2, 1, half * G * dk)

    wa, wb = expand_w1(w1[:, :half]), expand_w1(w1[:, half:])
    pa, pb = expand_pos(pos[:, :half]), expand_pos(pos[:, half:])
    w2bd = jnp.einsum("cde,gh->cgdhe", w2, eye).reshape(2, G * dk, G * dk).astype(BF16)
    tk = min(tk, kdim)
    assert kdim % tk == 0
    return pl.pallas_call(
        _cmp_kernel,
        grid=(2, bsz, kdim // tk),
        in_specs=[
            pl.BlockSpec((1, 1, ngrp, tk), lambda c, b, l: (c, b, 0, l)),
            pl.BlockSpec((1, 1, tk), lambda c, b, l: (c, 0, l)),
            pl.BlockSpec((1, 1, tk), lambda c, b, l: (c, 0, l)),
            pl.BlockSpec((1, tk, G * dk), lambda c, b, l: (c, l, 0)),
            pl.BlockSpec((1, tk, G * dk), lambda c, b, l: (c, l, 0)),
            pl.BlockSpec((1, G * dk, G * dk), lambda c, b, l: (c, 0, 0)),
            pl.BlockSpec((1, dk), lambda c, b, l: (0, 0)),
        ],
        out_specs=pl.BlockSpec((1, 1, ngrp, G * dk), lambda c, b, l: (c, b, 0, 0)),
        out_shape=jax.ShapeDtypeStruct((2, bsz, ngrp, G * dk), F32),
        scratch_shapes=[pltpu.VMEM((ngrp, G * dk), F32), pltpu.VMEM((ngrp, G * dk), F32)],
        compiler_params=_cparams(("parallel", "parallel", "arbitrary")),
        name="nsa_compress",
    )(xkv, pa, pb, wa, wb, w2bd, kg0.reshape(1, dk))


def _nsa_attn_kernel(q_ref, ks_ref, vs_ref, kw_ref, vw_ref, kc_ref, vc_ref, gate_ref, qg_ref, kg_ref, o_ref,
                     ksn_ref, vst_ref, kwn_ref, vwt_ref, kcb_ref, vct_ref, sel_ref, *, n_top):
    TQ, TS, TW, DK, HPG = NSA_TQ, NSA_TKS, NSA_TKW, NSA_DK, NSA_HPG
    i = pl.program_id(2)
    ncmp = kcb_ref.shape[0]
    nsel = sel_ref.shape[0]
    rows = HPG * TQ

    @pl.when(i == 0)
    def _():
        def norm_key(kt, gidx):
            ms = jnp.mean(kt * kt, axis=1, keepdims=True)
            return (kt * lax.rsqrt(ms + EPS) * kg_ref[gidx:gidx + 1, :]).astype(BF16)

        def prep_sel(t, carry):
            sl = pl.ds(pl.multiple_of(t * TS, TS), TS)
            ksn_ref[t] = norm_key(ks_ref[sl, :], 1)
            vst_ref[t] = vs_ref[sl, :].T.astype(BF16)
            return carry

        def prep_win(t, carry):
            sl = pl.ds(pl.multiple_of(t * TW, TW), TW)
            kwn_ref[t] = norm_key(kw_ref[sl, :], 2)
            vwt_ref[t] = vw_ref[sl, :].T.astype(BF16)
            return carry

        lax.fori_loop(0, ksn_ref.shape[0], prep_sel, 0)
        lax.fori_loop(0, kwn_ref.shape[0], prep_win, 0)
        kcb_ref[...] = kc_ref[0, 0].astype(BF16)
        vct_ref[...] = vc_ref[0, 0].T.astype(BF16)

    s0 = i * TQ
    q = q_ref[...]
    scale = DK ** -0.5
    qparts = []
    for r in range(HPG):
        qr = q[:, r * DK:(r + 1) * DK]
        ms = jnp.mean(qr * qr, axis=1, keepdims=True)
        qparts.append(qr * lax.rsqrt(ms + EPS) * (qg_ref[...] * scale))
    qrows = jnp.concatenate(qparts, axis=0).astype(BF16)
    tq = s0 + (_iota2((1, rows), 1) % TQ)

    sc = _dot_nt(kcb_ref[...], qrows)
    c_end = _iota2((ncmp, 1), 0) * NSA_STRIDE + (NSA_CMP_BLOCK - 1)
    maskc = c_end <= tq
    sc = jnp.where(maskc, sc, NEG)
    e = jnp.where(maskc, jnp.exp(sc - jnp.max(sc, axis=0, keepdims=True)), 0.0)
    pc = e / jnp.maximum(jnp.sum(e, axis=0, keepdims=True), 1e-30)
    oc = _dot(vct_ref[...], pc.astype(BF16))

    pcs = pc[:, 0:TQ]
    for r in range(1, HPG):
        pcs = pcs + pc[:, r * TQ:(r + 1) * TQ]
    jj = _iota2((nsel, ncmp), 0) * NSA_SEL
    cs = _iota2((nsel, ncmp), 1) * NSA_STRIDE
    overlap = ((cs < jj + NSA_SEL) & (cs + (NSA_CMP_BLOCK - 1) >= jj)).astype(F32)
    imp = _dot(overlap, pcs, precision=HIGHEST)
    jcol = _iota2((nsel, TQ), 0)
    cur = (s0 + _iota2((nsel, TQ), 1)) // NSA_SEL
    forced = (jcol == 0) | (jcol == cur) | (jcol == cur - 1)
    imp = jnp.where(jcol <= cur, imp + jnp.where(forced, FORCE_BONUS, 0.0), NEG)
    rank = jnp.zeros((nsel, TQ), F32)
    for j in range(nsel):
        vj = imp[j:j + 1, :]
        beats = (vj > imp) | ((vj == imp) & (jcol > j))
        rank = rank + beats.astype(F32)
    sel_ref[...] = jnp.where(rank < n_top, 0.0, NEG)

    bpt = TS // NSA_SEL

    def sel_scores(kt):
        s = _dot_nt(ksn_ref[kt], qrows)
        parts = []
        for jb in range(bpt):
            srow = sel_ref[pl.ds(kt * bpt + jb, 1), :]
            srow = jnp.concatenate([srow] * HPG, axis=1)
            parts.append(jnp.broadcast_to(srow, (NSA_SEL, rows)))
        return s + jnp.concatenate(parts, axis=0)

    def online(s, carry, vt):
        m, l, acc = carry
        m_new = jnp.maximum(m, jnp.max(s, axis=0, keepdims=True))
        alpha = jnp.exp(m - m_new)
        pexp = jnp.exp(s - m_new)
        l = alpha * l + jnp.sum(pexp, axis=0, keepdims=True)
        acc = alpha * acc + _dot(vt, pexp.astype(BF16))
        return m_new, l, acc

    def sel_body(kt, carry):
        return online(sel_scores(kt), carry, vst_ref[kt])

    init = (jnp.full((1, rows), NEG, F32), jnp.zeros((1, rows), F32), jnp.zeros((DK, rows), F32))
    kd = s0 // TS
    carry = lax.fori_loop(0, kd, sel_body, init)
    kpos = kd * TS + _iota2((TS, 1), 0)
    _, l_s, acc_s = online(jnp.where(kpos <= tq, sel_scores(kd), NEG), carry, vst_ref[kd])

    nwt = (NSA_WINDOW + TQ) // TW
    kt0 = i - NSA_WINDOW // TW
    idx = [jnp.maximum(kt0 + d, 0) for d in range(nwt)]
    sw = _dot_nt(jnp.concatenate([kwn_ref[ix] for ix in idx], axis=0), qrows)
    kposw = kt0 * TW + _iota2((nwt * TW, 1), 0)
    sw = jnp.where((kposw <= tq) & (kposw > tq - NSA_WINDOW) & (kposw >= 0), sw, NEG)
    pw = jnp.exp(sw - jnp.max(sw, axis=0, keepdims=True))
    l_w = jnp.sum(pw, axis=0, keepdims=True)
    pwb = pw.astype(BF16)
    acc_w = _dot(vwt_ref[idx[0]], pwb[0:TW])
    for d in range(1, nwt):
        acc_w = acc_w + _dot(vwt_ref[idx[d]], pwb[d * TW:(d + 1) * TW])

    gt = _sigmoid(gate_ref[0, 0])

    def grow(c):
        return jnp.concatenate([gt[c * HPG + r:c * HPG + r + 1, :] for r in range(HPG)], axis=1)

    out = (grow(0) * oc + grow(1) * (acc_s / jnp.maximum(l_s, 1e-30))
           + grow(2) * (acc_w / jnp.maximum(l_w, 1e-30)))
    for r in range(HPG):
        o_ref[:, r * DK:(r + 1) * DK] = out[:, r * TQ:(r + 1) * TQ].T.astype(o_ref.dtype)


def nsa_attention(proj, kvc, gate, q_g, k_g, bsz, seq):
    G, HPG, DK, TQ, TS, TW = NSA_G, NSA_HPG, NSA_DK, NSA_TQ, NSA_TKS, NSA_TKW
    assert TQ == TW and seq % TS == 0
    t = proj.shape[0]
    nq = seq // TQ
    ncmp = kvc.shape[2]
    nsel = seq // NSA_SEL
    n_top = min(NSA_TOP_N, nsel)
    qw = HPG * DK
    kv0 = G * HPG

    def kvspec(which):
        return pl.BlockSpec((seq, DK), lambda b, g, i: (b, kv0 + which * G + g))

    return pl.pallas_call(
        functools.partial(_nsa_attn_kernel, n_top=n_top),
        grid=(bsz, G, nq),
        in_specs=[
            pl.BlockSpec((TQ, qw), lambda b, g, i: (b * nq + i, g)),
            kvspec(2), kvspec(3), kvspec(4), kvspec(5),
            pl.BlockSpec((1, 1, ncmp, DK), lambda b, g, i: (0, b, 0, g)),
            pl.BlockSpec((1, 1, ncmp, DK), lambda b, g, i: (1, b, 0, g)),
            pl.BlockSpec((1, 1, 4 * HPG, TQ), lambda b, g, i: (b, g, 0, i)),
            pl.BlockSpec((1, DK), lambda b, g, i: (0, 0)),
            pl.BlockSpec((3, DK), lambda b, g, i: (0, 0)),
        ],
        out_specs=pl.BlockSpec((TQ, qw), lambda b, g, i: (b * nq + i, g)),
        out_shape=jax.ShapeDtypeStruct((t, G * qw), BF16),
        scratch_shapes=[
            pltpu.VMEM((seq // TS, TS, DK), BF16),
            pltpu.VMEM((seq // TS, DK, TS), BF16),
            pltpu.VMEM((seq // TW, TW, DK), BF16),
            pltpu.VMEM((seq // TW, DK, TW), BF16),
            pltpu.VMEM((ncmp, DK), BF16),
            pltpu.VMEM((DK, ncmp), BF16),
            pltpu.VMEM((nsel, TQ), F32),
        ],
        compiler_params=_cparams(("parallel", "parallel", "arbitrary")),
        name="nsa_attention",
    )(proj, proj, proj, proj, proj, kvc, kvc, gate, q_g.reshape(1, DK), k_g)


def nsa_layer(h, g_mix, w_in, q_g, k_g, cmp_pos, cmp_w1, cmp_w2, w_out, bsz, seq):
    G, HPG, DK = NSA_G, NSA_HPG, NSA_DK
    H = G * HPG
    nmain = H * DK + 6 * G * DK
    proj = norm_matmul(h, g_mix, w_in[:, :nmain].astype(BF16))
    gate_raw = norm_matmul(h, g_mix, _pad_cols(w_in[:, nmain:], LANES).astype(BF16))[:, :3 * H]
    ngrp = seq // NSA_STRIDE
    kvw = G * DK
    xkv = jnp.stack([
        proj[:, H * DK:H * DK + kvw].reshape(bsz, ngrp, NSA_STRIDE * kvw),
        proj[:, H * DK + kvw:H * DK + 2 * kvw].reshape(bsz, ngrp, NSA_STRIDE * kvw),
    ])
    kvc = nsa_compress(xkv, cmp_pos, cmp_w1, cmp_w2, k_g[0])
    gate = gate_raw.reshape(bsz, seq, G, HPG, 3).transpose(0, 2, 4, 3, 1).reshape(bsz, G, 3 * HPG, seq)
    gate = jnp.pad(gate, ((0, 0), (0, 0), (0, HPG), (0, 0)))
    o = nsa_attention(proj, kvc, gate, q_g, k_g, bsz, seq)
    return matmul_residual(o, w_out.astype(BF16), h)


def kernel(x, p, norm_mix_g, norm_mlp_g, mlp_w1, mlp_w2, ple_norm_g, ple_w_gate, ple_w_proj, nsa_w_in, nsa_q_g, nsa_k_g, nsa_cmp_pos, nsa_cmp_w1, nsa_cmp_w2, nsa_w_out, m2_w_in, m2_conv_w, m2_conv_b, m2_dt_bias, m2_a_log, m2_d, m2_norm_g, m2_w_out, gdn_w_in, gdn_conv_w, gdn_a_log, gdn_dt_bias, gdn_norm_g, gdn_w_out, ml_w_in, ml_gate_b, ml_norm_g, ml_w_out):
    bsz, seq, d = x.shape
    h = x.reshape(bsz * seq, d)
    for i in range(p.shape[0]):
        kind, j = i % 4, i // 4
        if kind == 0:
            h = nsa_layer(h, norm_mix_g[i], nsa_w_in[j], nsa_q_g[j], nsa_k_g[j], nsa_cmp_pos[j], nsa_cmp_w1[j], nsa_cmp_w2[j], nsa_w_out[j], bsz, seq)
        elif kind == 1:
            h = mamba2_layer(h, norm_mix_g[i], m2_w_in[j], m2_conv_w[j], m2_conv_b[j], m2_dt_bias[j], m2_a_log[j], m2_d[j], m2_norm_g[j], m2_w_out[j], bsz, seq)
        elif kind == 2:
            h = gdn_layer(h, norm_mix_g[i], gdn_w_in[j], gdn_conv_w[j], gdn_a_log[j], gdn_dt_bias[j], gdn_norm_g[j], gdn_w_out[j], bsz, seq)
        elif kind == 3:
            h = mlstm_layer(h, norm_mix_g[i], ml_w_in[j], ml_gate_b[j], ml_norm_g[j], ml_w_out[j], bsz, seq)
        h = mlp_block(h, norm_mlp_g[i], mlp_w1[i].astype(BF16), mlp_w2[i].astype(BF16))
        h = ple_update(h, ple_norm_g[i], ple_w_gate[i].astype(BF16), p[i].reshape(bsz * seq, -1), ple_w_proj[i].astype(BF16))
    return h.reshape(bsz, seq, d)
```

```python
import functools
import math

import jax
import jax.numpy as jnp
from jax import lax
from jax.experimental import pallas as pl
from jax.experimental.pallas import tpu as pltpu

F32 = jnp.float32
BF16 = jnp.bfloat16
EPS = 1e-6
NEG = -1e30
HIGHEST = lax.Precision.HIGHEST

VMEM_LIMIT_BYTES = 56 * 1024 * 1024
LANES = 128


def _cparams(sem):
    return pltpu.CompilerParams(dimension_semantics=sem, vmem_limit_bytes=VMEM_LIMIT_BYTES)


def _silu(x):
    return x * (1.0 / (1.0 + jnp.exp(-x)))


def _sigmoid(x):
    return 1.0 / (1.0 + jnp.exp(-x))


def _softplus(x):
    return jnp.maximum(x, 0.0) + jnp.log(1.0 + jnp.exp(-jnp.abs(x)))


def _dot(a, b, **kw):
    return jnp.dot(a, b, preferred_element_type=F32, **kw)


def _dot_nt(a, b, **kw):
    return lax.dot_general(a, b, (((1,), (1,)), ((), ())), preferred_element_type=F32, **kw)


def _dot_tn(a, b, **kw):
    return lax.dot_general(a, b, (((0,), (0,)), ((), ())), preferred_element_type=F32, **kw)


def _norm_mm_kernel(x_ref, g_ref, w_ref, o_ref, xn_ref):
    @pl.when(pl.program_id(1) == 0)
    def _():
        x = x_ref[...]
        ms = jnp.mean(x * x, axis=-1, keepdims=True)
        xn_ref[...] = (x * lax.rsqrt(ms + EPS) * g_ref[...]).astype(BF16)

    o_ref[...] = _dot(xn_ref[...], w_ref[...]).astype(o_ref.dtype)


def _tile(dim, want):
    t = min(want, dim)
    while dim % t:
        t //= 2
    return t


def norm_matmul(x, g, w, *, out_dtype=F32, tm=1024, tn=1024):
    m, k = x.shape
    n = w.shape[1]
    tm, tn = _tile(m, tm), _tile(n, tn)
    return pl.pallas_call(
        _norm_mm_kernel,
        grid=(m // tm, n // tn),
        in_specs=[
            pl.BlockSpec((tm, k), lambda i, j: (i, 0)),
            pl.BlockSpec((1, k), lambda i, j: (0, 0)),
            pl.BlockSpec((k, tn), lambda i, j: (0, j)),
        ],
        out_specs=pl.BlockSpec((tm, tn), lambda i, j: (i, j)),
        out_shape=jax.ShapeDtypeStruct((m, n), out_dtype),
        scratch_shapes=[pltpu.VMEM((tm, k), BF16)],
        compiler_params=_cparams(("parallel", "arbitrary")),
        name="norm_matmul",
    )(x, g.reshape(1, k), w)


def _mm_res_kernel(x_ref, w_ref, r_ref, o_ref):
    o_ref[...] = r_ref[...] + _dot(x_ref[...], w_ref[...])


def matmul_residual(x, w, res, *, tm=1024, tn=512):
    m, k = x.shape
    n = w.shape[1]
    tm, tn = _tile(m, tm), _tile(n, tn)
    return pl.pallas_call(
        _mm_res_kernel,
        grid=(m // tm, n // tn),
        in_specs=[
            pl.BlockSpec((tm, k), lambda i, j: (i, 0)),
            pl.BlockSpec((k, tn), lambda i, j: (0, j)),
            pl.BlockSpec((tm, tn), lambda i, j: (i, j)),
        ],
        out_specs=pl.BlockSpec((tm, tn), lambda i, j: (i, j)),
        out_shape=jax.ShapeDtypeStruct((m, n), F32),
        compiler_params=_cparams(("parallel", "arbitrary")),
        name="matmul_residual",
    )(x, w, res)


def _ple_kernel(x_ref, g_ref, wg_ref, p_ref, wp_ref, r_ref, o_ref, xn_ref):
    @pl.when(pl.program_id(1) == 0)
    def _():
        x = x_ref[...]
        ms = jnp.mean(x * x, axis=-1, keepdims=True)
        xn_ref[...] = (x * lax.rsqrt(ms + EPS) * g_ref[...]).astype(BF16)

    gate = _sigmoid(_dot(xn_ref[...], wg_ref[...]))
    proj = _dot(p_ref[...].astype(BF16), wp_ref[...])
    o_ref[...] = r_ref[...] + gate * proj


def ple_update(h, g, wg, p, wp, *, tm=1024, tn=512):
    m, k = h.shape
    n = wg.shape[1]
    kp = p.shape[1]
    tm, tn = _tile(m, tm), _tile(n, tn)
    return pl.pallas_call(
        _ple_kernel,
        grid=(m // tm, n // tn),
        in_specs=[
            pl.BlockSpec((tm, k), lambda i, j: (i, 0)),
            pl.BlockSpec((1, k), lambda i, j: (0, 0)),
            pl.BlockSpec((k, tn), lambda i, j: (0, j)),
            pl.BlockSpec((tm, kp), lambda i, j: (i, 0)),
            pl.BlockSpec((kp, tn), lambda i, j: (0, j)),
            pl.BlockSpec((tm, tn), lambda i, j: (i, j)),
        ],
        out_specs=pl.BlockSpec((tm, tn), lambda i, j: (i, j)),
        out_shape=jax.ShapeDtypeStruct((m, n), F32),
        scratch_shapes=[pltpu.VMEM((tm, k), BF16)],
        compiler_params=_cparams(("parallel", "arbitrary")),
        name="ple_update",
    )(h, g.reshape(1, k), wg, p, wp, h)


def _mlp_kernel(x_ref, g_ref, w1_ref, w2_ref, o_ref, xn_ref):
    @pl.when(pl.program_id(1) == 0)
    def _():
        x = x_ref[...]
        ms = jnp.mean(x * x, axis=-1, keepdims=True)
        xn_ref[...] = (x * lax.rsqrt(ms + EPS) * g_ref[...]).astype(BF16)
        o_ref[...] = x

    a = jnp.maximum(_dot(xn_ref[...], w1_ref[...]), 0.0)
    o_ref[...] += _dot((a * a).astype(BF16), w2_ref[...])


def mlp_block(h, g, w1, w2, *, tm=1024, tf=512):
    m, k = h.shape
    f = w1.shape[1]
    tm, tf = _tile(m, tm), _tile(f, tf)
    return pl.pallas_call(
        _mlp_kernel,
        grid=(m // tm, f // tf),
        in_specs=[
            pl.BlockSpec((tm, k), lambda i, j: (i, 0)),
            pl.BlockSpec((1, k), lambda i, j: (0, 0)),
            pl.BlockSpec((k, tf), lambda i, j: (0, j)),
            pl.BlockSpec((tf, k), lambda i, j: (j, 0)),
        ],
        out_specs=pl.BlockSpec((tm, k), lambda i, j: (i, 0)),
        out_shape=jax.ShapeDtypeStruct((m, k), F32),
        scratch_shapes=[pltpu.VMEM((tm, k), BF16)],
        compiler_params=_cparams(("parallel", "arbitrary")),
        name="mlp_block",
    )(h, g.reshape(1, k), w1, w2)


def _pad_cols(w, n):
    return jnp.pad(w, ((0, 0), (0, n - w.shape[1])))


def _iota2(shape, dim):
    return lax.broadcasted_iota(jnp.int32, shape, dim)


def _col_to_row(col, eye):
    return jnp.sum(jnp.where(eye, col, 0.0), axis=0, keepdims=True)


def _log_sigmoid(x):
    return jnp.minimum(x, 0.0) - jnp.log(1.0 + jnp.exp(-jnp.abs(x)))


def _run_interleaved(gens, start=None):
    start = start or (0,) * len(gens)
    done = [False] * len(gens)
    tick = 0
    while not all(done):
        for i, g in enumerate(gens):
            if tick >= start[i] and not done[i]:
                try:
                    next(g)
                except StopIteration:
                    done[i] = True
        tick += 1


ML_HEADS = 4
ML_CHUNK = 64


def _mlstm_kernel(q_ref, k_ref, v_ref, og_ref, gt_ref, gb_ref, ng_ref, o_ref, c_ref, n_ref, m_ref, *, dq, dv):
    L = ML_CHUNK

    @pl.when(pl.program_id(1) == 0)
    def _():
        c_ref[...] = jnp.zeros_like(c_ref)
        n_ref[...] = jnp.zeros_like(n_ref)
        m_ref[...] = jnp.zeros_like(m_ref)

    row = _iota2((L, L), 0)
    col = _iota2((L, L), 1)
    incl = row >= col
    eye = row == col
    gates = gt_ref[...] + gb_ref[...]
    scale = dq ** -0.5
    def head_stages(h):
        q = q_ref[:, h * dq:(h + 1) * dq] * scale
        k = k_ref[:, h * dq:(h + 1) * dq]
        v = v_ref[:, h * dv:(h + 1) * dv]
        li_c = gates[:, h:h + 1]
        lf_c = _log_sigmoid(gates[:, ML_HEADS + h:ML_HEADS + h + 1])
        yield
        li_r = _col_to_row(li_c, eye)
        lf_r = _col_to_row(lf_c, eye)
        yield
        b_c = jnp.sum(jnp.where(incl, lf_r, 0.0), axis=1, keepdims=True)
        yield
        b_r = _col_to_row(b_c, eye)
        m_prev = m_ref[h]
        yield
        dlog = jnp.where(incl, b_c - b_r + li_r, NEG)
        inter = b_c + m_prev
        m_t = jnp.maximum(inter, jnp.max(dlog, axis=1, keepdims=True))
        w_inter = jnp.exp(inter - m_t)
        qb = q.astype(BF16)
        kb = k.astype(BF16)
        vb = v.astype(BF16)
        yield
        s = _dot_nt(qb, kb) * jnp.where(incl, jnp.exp(dlog - m_t), 0.0)
        c_prev = c_ref[h]
        yield
        num = _dot(s.astype(BF16), vb) + w_inter * _dot(qb, c_prev.astype(BF16))
        qn = jnp.sum(q * n_ref[h], axis=1, keepdims=True)
        den = jnp.sum(s, axis=1, keepdims=True) + w_inter * qn
        yield
        hc = num / jnp.maximum(jnp.abs(den), jnp.exp(-m_t))
        tot = b_c[L - 1:L, :]
        ulog_c = tot - b_c + li_c
        m_new = jnp.maximum(tot + m_prev, jnp.max(ulog_c, axis=0, keepdims=True))
        wu_c = jnp.exp(ulog_c - m_new)
        dec = jnp.exp(tot + m_prev - m_new)
        kw = k * wu_c
        yield
        c_ref[h] = dec * c_prev + _dot_tn(kw.astype(BF16), vb)
        n_ref[h] = dec * n_ref[h] + jnp.sum(kw, axis=0, keepdims=True)
        m_ref[h] = m_new
        yield
        ms = jnp.mean(hc * hc, axis=1, keepdims=True)
        hn = hc * lax.rsqrt(ms + EPS) * ng_ref[:, h * dv:(h + 1) * dv]
        o_ref[:, h * dv:(h + 1) * dv] = (_sigmoid(og_ref[:, h * dv:(h + 1) * dv]) * hn).astype(o_ref.dtype)

    _run_interleaved([head_stages(h) for h in range(ML_HEADS)])


def mlstm_core(proj, gates, gate_b, norm_g, bsz, seq):
    H, L = ML_HEADS, ML_CHUNK
    t = proj.shape[0]
    dq = proj.shape[1] // (6 * H)
    dv = 2 * dq
    nc = seq // L
    gb = jnp.pad(gate_b, (0, LANES - gate_b.shape[0])).reshape(1, LANES)
    rowmap = lambda b, c: b * nc + c
    return pl.pallas_call(
        functools.partial(_mlstm_kernel, dq=dq, dv=dv),
        grid=(bsz, nc),
        in_specs=[
            pl.BlockSpec((L, H * dq), lambda b, c: (rowmap(b, c), 0)),
            pl.BlockSpec((L, H * dq), lambda b, c: (rowmap(b, c), 1)),
            pl.BlockSpec((L, H * dv), lambda b, c: (rowmap(b, c), 1)),
            pl.BlockSpec((L, H * dv), lambda b, c: (rowmap(b, c), 2)),
            pl.BlockSpec((L, LANES), lambda b, c: (rowmap(b, c), 0)),
            pl.BlockSpec((1, LANES), lambda b, c: (0, 0)),
            pl.BlockSpec((1, H * dv), lambda b, c: (0, 0)),
        ],
        out_specs=pl.BlockSpec((L, H * dv), lambda b, c: (rowmap(b, c), 0)),
        out_shape=jax.ShapeDtypeStruct((t, H * dv), BF16),
        scratch_shapes=[
            pltpu.VMEM((H, dq, dv), F32),
            pltpu.VMEM((H, 1, dq), F32),
            pltpu.VMEM((H, 1, 1), F32),
        ],
        compiler_params=_cparams(("parallel", "arbitrary")),
        name="mlstm_core",
    )(proj, proj, proj, proj, gates, gb, norm_g.reshape(1, H * dv))


def mlstm_layer(h, g_mix, w_in, gate_b, norm_g, w_out, bsz, seq):
    nmain = w_in.shape[1] - 2 * ML_HEADS
    proj = norm_matmul(h, g_mix, w_in[:, :nmain].astype(BF16))
    gates = norm_matmul(h, g_mix, _pad_cols(w_in[:, nmain:], LANES).astype(BF16))
    y = mlstm_core(proj, gates, gate_b, norm_g, bsz, seq)
    return matmul_residual(y, w_out.astype(BF16), h)


M2_CHUNK = 128
M2_GROUPS = 8
M2_STATE = 128
M2_HEAD_DIM = 64
CONV_K = 4
CARRY_ROWS = 8


def _causal_conv_silu(x_raw, carry_ref, w, b, cols=slice(None)):
    n = x_raw.shape[0]
    xe = jnp.concatenate([carry_ref[:, cols], x_raw], axis=0)
    acc = x_raw * w[CONV_K - 1:CONV_K, :]
    if b is not None:
        acc = acc + b
    for k in range(CONV_K - 1):
        sh = CONV_K - 1 - k
        acc = acc + pltpu.roll(xe, sh, axis=0)[CARRY_ROWS:, :] * w[k:k + 1, :]
    carry_ref[:, cols] = x_raw[n - CARRY_ROWS:, :]
    return _silu(acc)


def _ssd_kernel(z_ref, xs_ref, bm_ref, cm_ref, dtc_ref, dtr_ref, cwx_ref, cwb_ref, cwc_ref, cbx_ref, cbb_ref, cbc_ref,
                dtbc_ref, dtbr_ref, alogc_ref, alogr_ref, dsk_ref, ng_ref, o_ref, st_ref, cx_ref, cbm_ref, ccm_ref):
    L, P = M2_CHUNK, M2_HEAD_DIM
    PW = 2 * P
    npair = st_ref.shape[1] // PW
    assert L == PW == st_ref.shape[0]

    @pl.when(pl.program_id(2) == 0)
    def _():
        st_ref[...] = jnp.zeros_like(st_ref)
        cx_ref[...] = jnp.zeros_like(cx_ref)
        cbm_ref[...] = jnp.zeros_like(cbm_ref)
        ccm_ref[...] = jnp.zeros_like(ccm_ref)

    xs = _causal_conv_silu(xs_ref[...], cx_ref, cwx_ref[...], cbx_ref[...])
    bm = _causal_conv_silu(bm_ref[...], cbm_ref, cwb_ref[...], cbb_ref[...])
    cm = _causal_conv_silu(cm_ref[...], ccm_ref, cwc_ref[...], cbc_ref[...])
    dt_c = _softplus(dtc_ref[0, 0] + dtbc_ref[0])
    dt_r = _softplus(dtr_ref[0, 0] + dtbr_ref[0])
    lac_c = dt_c * (-jnp.exp(alogc_ref[0]))
    lac_r = dt_r * (-jnp.exp(alogr_ref[0]))
    row = _iota2((L, L), 0)
    col = _iota2((L, L), 1)
    tri = row >= col
    cum_c = _dot(tri.astype(F32), lac_c, precision=HIGHEST)
    cum_r = _dot(lac_r, (row <= col).astype(F32), precision=HIGHEST)
    cl_c = cum_c[L - 1:L, :]
    cl_r = cum_r[:, L - 1:L]
    cb = _dot_nt(cm.astype(BF16), bm.astype(BF16))
    bmt = bm.T
    st_all = st_ref[...]
    yi = _dot(cm.astype(BF16), st_all.astype(BF16))
    first = _iota2((1, PW), 1) < P

    prs = range(npair)
    colb = [[jnp.broadcast_to(cum_c[:, h:h + 1], (L, L)) for h in (2 * j, 2 * j + 1)] for j in prs]
    rowv = [[cum_r[h:h + 1, :] for h in (2 * j, 2 * j + 1)] for j in prs]
    dtv = [[dt_r[h:h + 1, :] for h in (2 * j, 2 * j + 1)] for j in prs]
    dec = [[jnp.exp(jnp.where(tri, colb[j][i] - rowv[j][i], NEG)) for i in range(2)] for j in prs]
    mm = [[cb * dec[j][i] * dtv[j][i] for i in range(2)] for j in prs]
    ww = [[bmt * (dtv[j][i] * jnp.exp(cl_r[2 * j + i:2 * j + i + 1, :] - rowv[j][i])) for i in range(2)] for j in prs]
    lhs = [jnp.concatenate([jnp.concatenate(mm[j], axis=1), jnp.concatenate(ww[j], axis=1)], axis=0).astype(BF16)
           for j in prs]
    xp = [xs[:, j * PW:(j + 1) * PW] for j in prs]
    rhs = [jnp.concatenate([jnp.where(first, xp[j], 0.0), jnp.where(first, 0.0, xp[j])], axis=0).astype(BF16)
           for j in prs]
    out = [_dot(lhs[j], rhs[j]) for j in prs]
    esc = [jnp.where(first, jnp.exp(colb[j][0]), jnp.exp(colb[j][1])) for j in prs]
    sdec = [jnp.where(first, jnp.exp(cl_c[:, 2 * j:2 * j + 1]), jnp.exp(cl_c[:, 2 * j + 1:2 * j + 2])) for j in prs]
    y = jnp.concatenate([out[j][:L] + yi[:, j * PW:(j + 1) * PW] * esc[j] for j in prs], axis=1)
    st_ref[...] = jnp.concatenate([st_all[:, j * PW:(j + 1) * PW] * sdec[j] + out[j][L:] for j in prs], axis=1)
    y = y + dsk_ref[...] * xs
    y = y * _silu(z_ref[...])
    ms = jnp.mean(y * y, axis=1, keepdims=True)
    o_ref[...] = (y * lax.rsqrt(ms + EPS) * ng_ref[...]).astype(o_ref.dtype)


def ssd_core(proj, dt_raw, conv_w, conv_b, dt_bias, a_log, d_skip, norm_g, bsz, seq):
    L, G, N, P = M2_CHUNK, M2_GROUPS, M2_STATE, M2_HEAD_DIM
    t = proj.shape[0]
    heads = dt_raw.shape[1]
    R = heads // G
    di = heads * P
    gw = di // G
    nc = seq // L
    dtc = dt_raw.reshape(bsz, seq, G, R).transpose(0, 2, 1, 3)
    dtr = dt_raw.reshape(bsz, seq, G, R).transpose(0, 2, 3, 1)
    cw = conv_w
    cbias = conv_b.reshape(1, -1)
    row = lambda b, g, c: b * nc + c
    nzx = di // gw
    nb0 = 2 * di // N
    cwb0 = di // N
    return pl.pallas_call(
        _ssd_kernel,
        grid=(bsz, G, nc),
        in_specs=[
            pl.BlockSpec((L, gw), lambda b, g, c: (row(b, g, c), g)),
            pl.BlockSpec((L, gw), lambda b, g, c: (row(b, g, c), nzx + g)),
            pl.BlockSpec((L, N), lambda b, g, c: (row(b, g, c), nb0 + g)),
            pl.BlockSpec((L, N), lambda b, g, c: (row(b, g, c), nb0 + G + g)),
            pl.BlockSpec((1, 1, L, R), lambda b, g, c: (b, g, c, 0)),
            pl.BlockSpec((1, 1, R, L), lambda b, g, c: (b, g, 0, c)),
            pl.BlockSpec((CONV_K, gw), lambda b, g, c: (0, g)),
            pl.BlockSpec((CONV_K, N), lambda b, g, c: (0, cwb0 + g)),
            pl.BlockSpec((CONV_K, N), lambda b, g, c: (0, cwb0 + G + g)),
            pl.BlockSpec((1, gw), lambda b, g, c: (0, g)),
            pl.BlockSpec((1, N), lambda b, g, c: (0, cwb0 + g)),
            pl.BlockSpec((1, N), lambda b, g, c: (0, cwb0 + G + g)),
            pl.BlockSpec((1, 1, R), lambda b, g, c: (g, 0, 0)),
            pl.BlockSpec((1, R, 1), lambda b, g, c: (g, 0, 0)),
            pl.BlockSpec((1, 1, R), lambda b, g, c: (g, 0, 0)),
            pl.BlockSpec((1, R, 1), lambda b, g, c: (g, 0, 0)),
            pl.BlockSpec((1, gw), lambda b, g, c: (0, g)),
            pl.BlockSpec((1, gw), lambda b, g, c: (0, g)),
        ],
        out_specs=pl.BlockSpec((L, gw), lambda b, g, c: (row(b, g, c), g)),
        out_shape=jax.ShapeDtypeStruct((t, di), BF16),
        scratch_shapes=[
            pltpu.VMEM((N, gw), F32),
            pltpu.VMEM((CARRY_ROWS, gw), F32),
            pltpu.VMEM((CARRY_ROWS, N), F32),
            pltpu.VMEM((CARRY_ROWS, N), F32),
        ],
        compiler_params=_cparams(("parallel", "parallel", "arbitrary")),
        name="ssd_core",
    )(proj, proj, proj, proj, dtc, dtr, cw, cw, cw, cbias, cbias, cbias,
      dt_bias.reshape(G, 1, R), dt_bias.reshape(G, R, 1), a_log.reshape(G, 1, R), a_log.reshape(G, R, 1),
      jnp.repeat(d_skip, P).reshape(1, di), norm_g.reshape(1, di))


def mamba2_layer(h, g_mix, w_in, conv_w, conv_b, dt_bias, a_log, d_skip, norm_g, w_out, bsz, seq):
    heads = dt_bias.shape[0]
    nmain = w_in.shape[1] - heads
    proj = norm_matmul(h, g_mix, w_in[:, :nmain].astype(BF16))
    dt_raw = norm_matmul(h, g_mix, _pad_cols(w_in[:, nmain:], LANES).astype(BF16))[:, :heads]
    y = ssd_core(proj, dt_raw, conv_w, conv_b, dt_bias, a_log, d_skip, norm_g, bsz, seq)
    return matmul_residual(y, w_out.astype(BF16), h)


GDN_CHUNK = 64
GDN_DH = 128
GDN_STACK = 4
GDN_STACKS = 8
GDN_STAGGER = (0, 0, 0, 0, 8, 8, 8, 8)


def _l2norm(x):
    return x * lax.rsqrt(jnp.sum(x * x, axis=1, keepdims=True) + EPS)


def _gdn_kernel(q_ref, k_ref, v_ref, z_ref, ba_ref, cwq_ref, cwk_ref, cwv_ref, alog_ref, dtb_ref, ng_ref,
                o_ref, st_ref, cq_ref, ck_ref, cv_ref):
    L, DH, HS = GDN_CHUNK, GDN_DH, GDN_STACK
    nv = GDN_STACK * GDN_STACKS
    SL = HS * L

    @pl.when(pl.program_id(2) == 0)
    def _():
        st_ref[...] = jnp.zeros_like(st_ref)
        cq_ref[...] = jnp.zeros_like(cq_ref)
        ck_ref[...] = jnp.zeros_like(ck_ref)
        cv_ref[...] = jnp.zeros_like(cv_ref)

    ba = ba_ref[0, 0]
    beta = _sigmoid(ba[:, :nv])
    gg = -jnp.exp(alog_ref[0]) * _softplus(ba[:, nv:2 * nv] + dtb_ref[0])

    ri = _iota2((SL, SL), 0)
    ci = _iota2((SL, SL), 1)
    same = (ri // L) == (ci // L)
    incl = same & (ri >= ci)
    strict = same & (ri > ci)
    eye = ri == ci
    last = same & ((ci % L) == L - 1)

    heads = range(HS)
    rows = [slice(i * L, (i + 1) * L) for i in heads]
    head_of_row = _iota2((SL, 1), 0) // L

    def stack_stages(s):
        hv0 = s * HS
        kc = slice((hv0 // 2) * DH, (hv0 // 2 + HS // 2) * DH)
        vc = slice(hv0 * DH, (hv0 + HS) * DH)
        q = _causal_conv_silu(q_ref[:, kc], cq_ref, cwq_ref[:, kc], None, kc)
        k = _causal_conv_silu(k_ref[:, kc], ck_ref, cwk_ref[:, kc], None, kc)
        yield
        v = _causal_conv_silu(v_ref[:, vc], cv_ref, cwv_ref[:, vc], None, vc)
        yield
        qn = [_l2norm(q[:, i * DH:(i + 1) * DH]) * DH ** -0.5 for i in range(HS // 2)]
        kn = [_l2norm(k[:, i * DH:(i + 1) * DH]) for i in range(HS // 2)]
        kst = jnp.concatenate([kn[i // 2] for i in heads], axis=0)
        qst = jnp.concatenate([qn[i // 2] for i in heads], axis=0)
        vst = jnp.concatenate([v[:, i * DH:(i + 1) * DH] for i in heads], axis=0)
        beta_c = jnp.concatenate([beta[:, hv0 + i:hv0 + i + 1] for i in heads], axis=0)
        g_c = jnp.concatenate([gg[:, hv0 + i:hv0 + i + 1] for i in heads], axis=0)
        yield
        g_r = _col_to_row(g_c, eye)
        yield
        cum_c = jnp.sum(jnp.where(incl, g_r, 0.0), axis=1, keepdims=True)
        yield
        cum_r = _col_to_row(cum_c, eye)
        yield
        dmask = jnp.exp(jnp.where(incl, cum_c - cum_r, NEG))
        kb = kst * beta_c
        kst_b = kst.astype(BF16)
        yield
        nmat = jnp.where(strict, _dot_nt(kb.astype(BF16), kst_b) * dmask, 0.0)
        ecum = jnp.exp(cum_c)
        rhs = jnp.concatenate([vst * beta_c, kb * ecum], axis=1)
        yield
        pw_b = nmat.astype(BF16)
        sol = rhs - _dot(pw_b, rhs.astype(BF16))
        yield
        for _ in range(5):
            pw_b = _dot(pw_b, pw_b).astype(BF16)
            sol = sol + _dot(pw_b, sol.astype(BF16))
            yield
        qe = qst * ecum
        attn = _dot_nt(qst.astype(BF16), kst_b) * dmask
        cl_c = jnp.sum(jnp.where(last, cum_r, 0.0), axis=1, keepdims=True)
        kd = kst * jnp.exp(cl_c - cum_c)
        st_all = st_ref[s]
        yield
        both = [_dot(jnp.concatenate([sol[rows[i], DH:], qe[rows[i]]], axis=0).astype(BF16),
                     st_all[:, i * DH:(i + 1) * DH].astype(BF16)) for i in heads]
        yield
        vnew = jnp.concatenate([sol[rows[i], :DH] - both[i][:L] for i in heads], axis=0)
        o = jnp.concatenate([both[i][L:] for i in heads], axis=0) + _dot(attn.astype(BF16), vnew.astype(BF16))
        yield
        vbd = jnp.concatenate([jnp.where(head_of_row == i, vnew, 0.0) for i in heads], axis=1)
        decay = jnp.concatenate(
            [jnp.broadcast_to(jnp.exp(cl_c[i * L:i * L + 1, :]), (1, DH)) for i in heads], axis=1)
        st_ref[s] = st_all * decay + _dot(kd.T.astype(BF16), vbd.astype(BF16))
        yield
        for i in heads:
            hv = hv0 + i
            oi = o[rows[i]]
            ms = jnp.mean(oi * oi, axis=1, keepdims=True)
            on = oi * lax.rsqrt(ms + EPS) * ng_ref[...]
            o_ref[:, hv * DH:(hv + 1) * DH] = (on * _silu(z_ref[:, hv * DH:(hv + 1) * DH])).astype(o_ref.dtype)

    _run_interleaved([stack_stages(s) for s in range(GDN_STACKS)], GDN_STAGGER)


def gdn_core(proj, ba_raw, conv_w, a_log, dt_bias, norm_g, bsz, seq):
    L, DH = GDN_CHUNK, GDN_DH
    nv = GDN_STACK * GDN_STACKS
    t = proj.shape[0]
    hv_total = ba_raw.shape[1] // 2
    ng = hv_total // nv
    qw = (nv // 2) * DH
    vw = nv * DH
    nc = seq // L
    b_part = ba_raw[:, :hv_total].reshape(bsz, seq, ng, nv)
    a_part = ba_raw[:, hv_total:].reshape(bsz, seq, ng, nv)
    ba = jnp.concatenate([b_part, a_part], axis=-1).transpose(0, 2, 1, 3)
    row = lambda b, g, c: b * nc + c
    kq0 = ng
    v0 = 2 * ng * qw // vw
    return pl.pallas_call(
        _gdn_kernel,
        grid=(bsz, ng, nc),
        in_specs=[
            pl.BlockSpec((L, qw), lambda b, g, c: (row(b, g, c), g)),
            pl.BlockSpec((L, qw), lambda b, g, c: (row(b, g, c), kq0 + g)),
            pl.BlockSpec((L, vw), lambda b, g, c: (row(b, g, c), v0 + g)),
            pl.BlockSpec((L, vw), lambda b, g, c: (row(b, g, c), v0 + ng + g)),
            pl.BlockSpec((1, 1, L, 2 * nv), lambda b, g, c: (b, g, c, 0)),
            pl.BlockSpec((CONV_K, qw), lambda b, g, c: (0, g)),
            pl.BlockSpec((CONV_K, qw), lambda b, g, c: (0, kq0 + g)),
            pl.BlockSpec((CONV_K, vw), lambda b, g, c: (0, v0 + g)),
            pl.BlockSpec((1, 1, nv), lambda b, g, c: (g, 0, 0)),
            pl.BlockSpec((1, 1, nv), lambda b, g, c: (g, 0, 0)),
            pl.BlockSpec((1, DH), lambda b, g, c: (0, 0)),
        ],
        out_specs=pl.BlockSpec((L, vw), lambda b, g, c: (row(b, g, c), g)),
        out_shape=jax.ShapeDtypeStruct((t, hv_total * DH), BF16),
        scratch_shapes=[
            pltpu.VMEM((GDN_STACKS, DH, GDN_STACK * DH), F32),
            pltpu.VMEM((CARRY_ROWS, qw), F32),
            pltpu.VMEM((CARRY_ROWS, qw), F32),
            pltpu.VMEM((CARRY_ROWS, vw), F32),
        ],
        compiler_params=_cparams(("parallel", "parallel", "arbitrary")),
        name="gdn_core",
    )(proj, proj, proj, proj, ba, conv_w, conv_w, conv_w,
      a_log.reshape(ng, 1, nv), dt_bias.reshape(ng, 1, nv), norm_g.reshape(1, DH))


def gdn_layer(h, g_mix, w_in, conv_w, a_log, dt_bias, norm_g, w_out, bsz, seq):
    hv = a_log.shape[0]
    nmain = w_in.shape[1] - 2 * hv
    proj = norm_matmul(h, g_mix, w_in[:, :nmain].astype(BF16))
    ba_raw = norm_matmul(h, g_mix, _pad_cols(w_in[:, nmain:], LANES).astype(BF16))[:, :2 * hv]
    y = gdn_core(proj, ba_raw, conv_w, a_log, dt_bias, norm_g, bsz, seq)
    return matmul_residual(y, w_out.astype(BF16), h)


NSA_DK = 128
NSA_G = 4
NSA_HPG = 4
NSA_STRIDE = 16
NSA_CMP_BLOCK = 32
NSA_SEL = 64
NSA_TOP_N = 16
NSA_WINDOW = 512
NSA_TQ = 128
NSA_TKS = 512
NSA_TKW = 128
FORCE_BONUS = 1e4


def _cmp_kernel(x_ref, pa_ref, pb_ref, wa_ref, wb_ref, w2_ref, kg_ref, o_ref, acca_ref, accb_ref):
    kk = pl.program_id(2)

    @pl.when(kk == 0)
    def _():
        acca_ref[...] = jnp.zeros_like(acca_ref)
        accb_ref[...] = jnp.zeros_like(accb_ref)

    x = x_ref[0, 0]
    acca_ref[...] += _dot((x + pa_ref[0]).astype(BF16), wa_ref[0])
    accb_ref[...] += _dot((x + pb_ref[0]).astype(BF16), wb_ref[0])

    @pl.when(kk == pl.num_programs(2) - 1)
    def _():
        nrow = acca_ref.shape[0]
        pre = acca_ref[...] + pltpu.roll(accb_ref[...], nrow - 1, axis=0)
        y = _dot(_silu(pre).astype(BF16), w2_ref[0])
        is_k = pl.program_id(0) == 0
        for g in range(NSA_G):
            yg = y[:, g * NSA_DK:(g + 1) * NSA_DK]
            ms = jnp.mean(yg * yg, axis=1, keepdims=True)
            yn = yg * lax.rsqrt(ms + EPS) * kg_ref[...]
            o_ref[0, 0, :, g * NSA_DK:(g + 1) * NSA_DK] = jnp.where(is_k, yn, yg)


def nsa_compress(xkv, pos, w1, w2, kg0, *, tk=2048):
    _, bsz, ngrp, kdim = xkv.shape
    G, dk, half = NSA_G, NSA_DK, NSA_CMP_BLOCK // 2
    eye = jnp.eye(G, dtype=w1.dtype)

    def expand_w1(w):
        return jnp.einsum("clde,gh->clgdhe", w, eye).reshape(2, half * G * dk, G * dk).astype(BF16)

    def expand_pos(ps):
        return jnp.broadcast_to(ps[:, :, None, :], (2, half, G, dk)).reshape(2, 1, half * G * dk)

    wa, wb = expand_w1(w1[:, :half]), expand_w1(w1[:, half:])
    pa, pb = expand_pos(pos[:, :half]), expand_pos(pos[:, half:])
    w2bd = jnp.einsum("cde,gh->cgdhe", w2, eye).reshape(2, G * dk, G * dk).astype(BF16)
    tk = min(tk, kdim)
    assert kdim % tk == 0
    return pl.pallas_call(
        _cmp_kernel,
        grid=(2, bsz, kdim // tk),
        in_specs=[
            pl.BlockSpec((1, 1, ngrp, tk), lambda c, b, l: (c, b, 0, l)),
            pl.BlockSpec((1, 1, tk), lambda c, b, l: (c, 0, l)),
            pl.BlockSpec((1, 1, tk), lambda c, b, l: (c, 0, l)),
            pl.BlockSpec((1, tk, G * dk), lambda c, b, l: (c, l, 0)),
            pl.BlockSpec((1, tk, G * dk), lambda c, b, l: (c, l, 0)),
            pl.BlockSpec((1, G * dk, G * dk), lambda c, b, l: (c, 0, 0)),
            pl.BlockSpec((1, dk), lambda c, b, l: (0, 0)),
        ],
        out_specs=pl.BlockSpec((1, 1, ngrp, G * dk), lambda c, b, l: (c, b, 0, 0)),
        out_shape=jax.ShapeDtypeStruct((2, bsz, ngrp, G * dk), F32),
        scratch_shapes=[pltpu.VMEM((ngrp, G * dk), F32), pltpu.VMEM((ngrp, G * dk), F32)],
        compiler_params=_cparams(("parallel", "parallel", "arbitrary")),
        name="nsa_compress",
    )(xkv, pa, pb, wa, wb, w2bd, kg0.reshape(1, dk))


def _nsa_attn_kernel(q_ref, ks_ref, vs_ref, kw_ref, vw_ref, kc_ref, vc_ref, gate_ref, qg_ref, kg_ref, o_ref,
                     ksn_ref, vst_ref, kwn_ref, vwt_ref, kcb_ref, vct_ref, sel_ref, *, n_top):
    TQ, TS, TW, DK, HPG = NSA_TQ, NSA_TKS, NSA_TKW, NSA_DK, NSA_HPG
    i = pl.program_id(2)
    ncmp = kcb_ref.shape[0]
    nsel = sel_ref.shape[0]
    rows = HPG * TQ

    @pl.when(i == 0)
    def _():
        def norm_key(kt, gidx):
            ms = jnp.mean(kt * kt, axis=1, keepdims=True)
            return (kt * lax.rsqrt(ms + EPS) * kg_ref[gidx:gidx + 1, :]).astype(BF16)

        def prep_sel(t, carry):
            sl = pl.ds(pl.multiple_of(t * TS, TS), TS)
            ksn_ref[t] = norm_key(ks_ref[sl, :], 1)
            vst_ref[t] = vs_ref[sl, :].T.astype(BF16)
            return carry

        def prep_win(t, carry):
            sl = pl.ds(pl.multiple_of(t * TW, TW), TW)
            kwn_ref[t] = norm_key(kw_ref[sl, :], 2)
            vwt_ref[t] = vw_ref[sl, :].T.astype(BF16)
            return carry

        lax.fori_loop(0, ksn_ref.shape[0], prep_sel, 0)
        lax.fori_loop(0, kwn_ref.shape[0], prep_win, 0)
        kcb_ref[...] = kc_ref[0, 0].astype(BF16)
        vct_ref[...] = vc_ref[0, 0].T.astype(BF16)

    s0 = i * TQ
    q = q_ref[...]
    scale = DK ** -0.5
    qparts = []
    for r in range(HPG):
        qr = q[:, r * DK:(r + 1) * DK]
        ms = jnp.mean(qr * qr, axis=1, keepdims=True)
        qparts.append(qr * lax.rsqrt(ms + EPS) * (qg_ref[...] * scale))
    qrows = jnp.concatenate(qparts, axis=0).astype(BF16)
    tq = s0 + (_iota2((1, rows), 1) % TQ)

    res = {}

    def compressed_stages():
        sc = _dot_nt(kcb_ref[...], qrows)
        c_end = _iota2((ncmp, 1), 0) * NSA_STRIDE + (NSA_CMP_BLOCK - 1)
        maskc = c_end <= tq
        yield
        sc = jnp.where(maskc, sc, NEG)
        e = jnp.where(maskc, jnp.exp(sc - jnp.max(sc, axis=0, keepdims=True)), 0.0)
        yield
        pc = e / jnp.maximum(jnp.sum(e, axis=0, keepdims=True), 1e-30)
        res["oc"] = _dot(vct_ref[...], pc.astype(BF16))
        yield
        pcs = pc[:, 0:TQ]
        for r in range(1, HPG):
            pcs = pcs + pc[:, r * TQ:(r + 1) * TQ]
        jj = _iota2((nsel, ncmp), 0) * NSA_SEL
        cs = _iota2((nsel, ncmp), 1) * NSA_STRIDE
        overlap = ((cs < jj + NSA_SEL) & (cs + (NSA_CMP_BLOCK - 1) >= jj)).astype(F32)
        imp = _dot(overlap, pcs, precision=HIGHEST)
        yield
        jcol = _iota2((nsel, TQ), 0)
        cur = (s0 + _iota2((nsel, TQ), 1)) // NSA_SEL
        forced = (jcol == 0) | (jcol == cur) | (jcol == cur - 1)
        imp = jnp.where(jcol <= cur, imp + jnp.where(forced, FORCE_BONUS, 0.0), NEG)
        rank = jnp.zeros((nsel, TQ), F32)
        for j in range(nsel):
            vj = imp[j:j + 1, :]
            beats = (vj > imp) | ((vj == imp) & (jcol > j))
            rank = rank + beats.astype(F32)
            if j % 8 == 7:
                yield
        sel_ref[...] = jnp.where(rank < n_top, 0.0, NEG)

    def window_stages():
        nwt = (NSA_WINDOW + TQ) // TW
        kt0 = i - NSA_WINDOW // TW
        idx = [jnp.maximum(kt0 + d, 0) for d in range(nwt)]
        sw = _dot_nt(jnp.concatenate([kwn_ref[ix] for ix in idx], axis=0), qrows)
        yield
        kofs = _iota2((TW, 1), 0)
        parts = []
        for d in range(nwt):
            kt = kt0 + d
            sd = sw[d * TW:(d + 1) * TW]
            if d == 0:
                sd = jnp.where((kt * TW + kofs > tq - NSA_WINDOW) & (kt >= 0), sd, NEG)
            elif d == nwt - 1:
                sd = jnp.where(kt * TW + kofs <= tq, sd, NEG)
            else:
                sd = sd + jnp.where(kt >= 0, 0.0, NEG)
            parts.append(sd)
        sw = jnp.concatenate(parts, axis=0)
        yield
        pw = jnp.exp(sw - jnp.max(sw, axis=0, keepdims=True))
        yield
        res["l_w"] = jnp.sum(pw, axis=0, keepdims=True)
        pwb = pw.astype(BF16)
        acc_w = _dot(vwt_ref[idx[0]], pwb[0:TW])
        for d in range(1, nwt):
            acc_w = acc_w + _dot(vwt_ref[idx[d]], pwb[d * TW:(d + 1) * TW])
        res["acc_w"] = acc_w

    _run_interleaved([compressed_stages(), window_stages()])
    oc, l_w, acc_w = res["oc"], res["l_w"], res["acc_w"]

    bpt = TS // NSA_SEL

    def sel_scores(kt):
        s = _dot_nt(ksn_ref[kt], qrows)
        parts = []
        for jb in range(bpt):
            srow = sel_ref[pl.ds(kt * bpt + jb, 1), :]
            srow = jnp.concatenate([srow] * HPG, axis=1)
            parts.append(jnp.broadcast_to(srow, (NSA_SEL, rows)))
        return s + jnp.concatenate(parts, axis=0)

    def online(s, carry, vt):
        m, l, acc = carry
        m_new = jnp.maximum(m, jnp.max(s, axis=0, keepdims=True))
        alpha = jnp.exp(m - m_new)
        pexp = jnp.exp(s - m_new)
        l = alpha * l + jnp.sum(pexp, axis=0, keepdims=True)
        acc = alpha * acc + _dot(vt, pexp.astype(BF16))
        return m_new, l, acc

    def sel_body(kt, carry):
        return online(sel_scores(kt), carry, vst_ref[kt])

    init = (jnp.full((1, rows), NEG, F32), jnp.zeros((1, rows), F32), jnp.zeros((DK, rows), F32))
    kd = s0 // TS
    carry = lax.fori_loop(0, kd, sel_body, init)
    kpos = kd * TS + _iota2((TS, 1), 0)
    _, l_s, acc_s = online(jnp.where(kpos <= tq, sel_scores(kd), NEG), carry, vst_ref[kd])

    gt = _sigmoid(gate_ref[0, 0])

    def grow(c):
        return jnp.concatenate([gt[c * HPG + r:c * HPG + r + 1, :] for r in range(HPG)], axis=1)

    out = (grow(0) * oc + grow(1) * (acc_s / jnp.maximum(l_s, 1e-30))
           + grow(2) * (acc_w / jnp.maximum(l_w, 1e-30)))
    for r in range(HPG):
        o_ref[:, r * DK:(r + 1) * DK] = out[:, r * TQ:(r + 1) * TQ].T.astype(o_ref.dtype)


def nsa_attention(proj, kvc, gate, q_g, k_g, bsz, seq):
    G, HPG, DK, TQ, TS, TW = NSA_G, NSA_HPG, NSA_DK, NSA_TQ, NSA_TKS, NSA_TKW
    assert TQ == TW and seq % TS == 0
    t = proj.shape[0]
    nq = seq // TQ
    ncmp = kvc.shape[2]
    nsel = seq // NSA_SEL
    n_top = min(NSA_TOP_N, nsel)
    qw = HPG * DK
    kv0 = G * HPG

    def kvspec(which):
        return pl.BlockSpec((seq, DK), lambda b, g, i: (b, kv0 + which * G + g))

    return pl.pallas_call(
        functools.partial(_nsa_attn_kernel, n_top=n_top),
        grid=(bsz, G, nq),
        in_specs=[
            pl.BlockSpec((TQ, qw), lambda b, g, i: (b * nq + i, g)),
            kvspec(2), kvspec(3), kvspec(4), kvspec(5),
            pl.BlockSpec((1, 1, ncmp, DK), lambda b, g, i: (0, b, 0, g)),
            pl.BlockSpec((1, 1, ncmp, DK), lambda b, g, i: (1, b, 0, g)),
            pl.BlockSpec((1, 1, 4 * HPG, TQ), lambda b, g, i: (b, g, 0, i)),
            pl.BlockSpec((1, DK), lambda b, g, i: (0, 0)),
            pl.BlockSpec((3, DK), lambda b, g, i: (0, 0)),
        ],
        out_specs=pl.BlockSpec((TQ, qw), lambda b, g, i: (b * nq + i, g)),
        out_shape=jax.ShapeDtypeStruct((t, G * qw), BF16),
        scratch_shapes=[
            pltpu.VMEM((seq // TS, TS, DK), BF16),
            pltpu.VMEM((seq // TS, DK, TS), BF16),
            pltpu.VMEM((seq // TW, TW, DK), BF16),
            pltpu.VMEM((seq // TW, DK, TW), BF16),
            pltpu.VMEM((ncmp, DK), BF16),
            pltpu.VMEM((DK, ncmp), BF16),
            pltpu.VMEM((nsel, TQ), F32),
        ],
        compiler_params=_cparams(("parallel", "parallel", "arbitrary")),
        name="nsa_attention",
    )(proj, proj, proj, proj, proj, kvc, kvc, gate, q_g.reshape(1, DK), k_g)


def nsa_layer(h, g_mix, w_in, q_g, k_g, cmp_pos, cmp_w1, cmp_w2, w_out, bsz, seq):
    G, HPG, DK = NSA_G, NSA_HPG, NSA_DK
    H = G * HPG
    nmain = H * DK + 6 * G * DK
    proj = norm_matmul(h, g_mix, w_in[:, :nmain].astype(BF16))
    gate_raw = norm_matmul(h, g_mix, _pad_cols(w_in[:, nmain:], LANES).astype(BF16))[:, :3 * H]
    ngrp = seq // NSA_STRIDE
    kvw = G * DK
    xkv = jnp.stack([
        proj[:, H * DK:H * DK + kvw].reshape(bsz, ngrp, NSA_STRIDE * kvw),
        proj[:, H * DK + kvw:H * DK + 2 * kvw].reshape(bsz, ngrp, NSA_STRIDE * kvw),
    ])
    kvc = nsa_compress(xkv, cmp_pos, cmp_w1, cmp_w2, k_g[0])
    gate = gate_raw.reshape(bsz, seq, G, HPG, 3).transpose(0, 2, 4, 3, 1).reshape(bsz, G, 3 * HPG, seq)
    gate = jnp.pad(gate, ((0, 0), (0, 0), (0, HPG), (0, 0)))
    o = nsa_attention(proj, kvc, gate, q_g, k_g, bsz, seq)
    return matmul_residual(o, w_out.astype(BF16), h)


def kernel(x, p, norm_mix_g, norm_mlp_g, mlp_w1, mlp_w2, ple_norm_g, ple_w_gate, ple_w_proj, nsa_w_in, nsa_q_g, nsa_k_g, nsa_cmp_pos, nsa_cmp_w1, nsa_cmp_w2, nsa_w_out, m2_w_in, m2_conv_w, m2_conv_b, m2_dt_bias, m2_a_log, m2_d, m2_norm_g, m2_w_out, gdn_w_in, gdn_conv_w, gdn_a_log, gdn_dt_bias, gdn_norm_g, gdn_w_out, ml_w_in, ml_gate_b, ml_norm_g, ml_w_out):
    bsz, seq, d = x.shape
    h = x.reshape(bsz * seq, d)
    for i in range(p.shape[0]):
        kind, j = i % 4, i // 4
        if kind == 0:
            h = nsa_layer(h, norm_mix_g[i], nsa_w_in[j], nsa_q_g[j], nsa_k_g[j], nsa_cmp_pos[j], nsa_cmp_w1[j], nsa_cmp_w2[j], nsa_w_out[j], bsz, seq)
        elif kind == 1:
            h = mamba2_layer(h, norm_mix_g[i], m2_w_in[j], m2_conv_w[j], m2_conv_b[j], m2_dt_bias[j], m2_a_log[j], m2_d[j], m2_norm_g[j], m2_w_out[j], bsz, seq)
        elif kind == 2:
            h = gdn_layer(h, norm_mix_g[i], gdn_w_in[j], gdn_conv_w[j], gdn_a_log[j], gdn_dt_bias[j], gdn_norm_g[j], gdn_w_out[j], bsz, seq)
        elif kind == 3:
            h = mlstm_layer(h, norm_mix_g[i], ml_w_in[j], ml_gate_b[j], ml_norm_g[j], ml_w_out[j], bsz, seq)
        h = mlp_block(h, norm_mlp_g[i], mlp_w1[i].astype(BF16), mlp_w2[i].astype(BF16))
        h = ple_update(h, ple_norm_g[i], ple_w_gate[i].astype(BF16), p[i].reshape(bsz * seq, -1), ple_w_proj[i].astype(BF16))
    return h.reshape(bsz, seq, d)
```

```python
import functools
import math

import jax
import jax.numpy as jnp
from jax import lax
from jax.experimental import pallas as pl
from jax.experimental.pallas import tpu as pltpu

F32 = jnp.float32
BF16 = jnp.bfloat16
EPS = 1e-6
NEG = -1e30
HIGHEST = lax.Precision.HIGHEST

VMEM_LIMIT_BYTES = 56 * 1024 * 1024
LANES = 128


def _cparams(sem):
    return pltpu.CompilerParams(dimension_semantics=sem, vmem_limit_bytes=VMEM_LIMIT_BYTES)


def _silu(x):
    return x * (1.0 / (1.0 + jnp.exp(-x)))


def _sigmoid(x):
    return 1.0 / (1.0 + jnp.exp(-x))


def _softplus(x):
    return jnp.maximum(x, 0.0) + jnp.log(1.0 + jnp.exp(-jnp.abs(x)))


def _dot(a, b, **kw):
    return jnp.dot(a, b, preferred_element_type=F32, **kw)


def _dot_nt(a, b, **kw):
    return lax.dot_general(a, b, (((1,), (1,)), ((), ())), preferred_element_type=F32, **kw)


def _dot_tn(a, b, **kw):
    return lax.dot_general(a, b, (((0,), (0,)), ((), ())), preferred_element_type=F32, **kw)


def _norm_mm_kernel(x_ref, g_ref, w_ref, o_ref, xn_ref):
    @pl.when(pl.program_id(1) == 0)
    def _():
        x = x_ref[...]
        ms = jnp.mean(x * x, axis=-1, keepdims=True)
        xn_ref[...] = (x * lax.rsqrt(ms + EPS) * g_ref[...]).astype(BF16)

    o_ref[...] = _dot(xn_ref[...], w_ref[...]).astype(o_ref.dtype)


def _tile(dim, want):
    t = min(want, dim)
    while dim % t:
        t //= 2
    return t


def norm_matmul(x, g, w, *, out_dtype=F32, tm=1024, tn=2048):
    m, k = x.shape
    n = w.shape[1]
    tm, tn = _tile(m, tm), _tile(n, tn)
    return pl.pallas_call(
        _norm_mm_kernel,
        grid=(m // tm, n // tn),
        in_specs=[
            pl.BlockSpec((tm, k), lambda i, j: (i, 0)),
            pl.BlockSpec((1, k), lambda i, j: (0, 0)),
            pl.BlockSpec((k, tn), lambda i, j: (0, j)),
        ],
        out_specs=pl.BlockSpec((tm, tn), lambda i, j: (i, j)),
        out_shape=jax.ShapeDtypeStruct((m, n), out_dtype),
        scratch_shapes=[pltpu.VMEM((tm, k), BF16)],
        compiler_params=_cparams(("parallel", "arbitrary")),
        name="norm_matmul",
    )(x, g.reshape(1, k), w)


def _norm_mm_t_kernel(x_ref, g_ref, w_ref, o_ref):
    x = x_ref[...]
    ms = jnp.mean(x * x, axis=-1, keepdims=True)
    xn = (x * lax.rsqrt(ms + EPS) * g_ref[...]).astype(BF16)
    o_ref[...] = _dot(xn, w_ref[...]).T


def norm_matmul_t(x, g, w, *, tm=1024):
    m, k = x.shape
    n = w.shape[1]
    assert n == LANES
    tm = _tile(m, tm)
    return pl.pallas_call(
        _norm_mm_t_kernel,
        grid=(m // tm,),
        in_specs=[
            pl.BlockSpec((tm, k), lambda i: (i, 0)),
            pl.BlockSpec((1, k), lambda i: (0, 0)),
            pl.BlockSpec((k, n), lambda i: (0, 0)),
        ],
        out_specs=pl.BlockSpec((n, tm), lambda i: (0, i)),
        out_shape=jax.ShapeDtypeStruct((n, m), F32),
        compiler_params=_cparams(("parallel",)),
        name="norm_matmul_t",
    )(x, g.reshape(1, k), w)


def _mm_res_kernel(x_ref, w_ref, r_ref, o_ref):
    o_ref[...] = r_ref[...] + _dot(x_ref[...], w_ref[...])


def matmul_residual(x, w, res, *, tm=1024, tn=1024):
    m, k = x.shape
    n = w.shape[1]
    tm, tn = _tile(m, tm), _tile(n, tn)
    while 2 * (tm * k * 2 + k * tn * 2 + 2 * tm * tn * 4) + tm * tn * 4 > 0.8 * VMEM_LIMIT_BYTES:
        tn //= 2
    return pl.pallas_call(
        _mm_res_kernel,
        grid=(m // tm, n // tn),
        in_specs=[
            pl.BlockSpec((tm, k), lambda i, j: (i, 0)),
            pl.BlockSpec((k, tn), lambda i, j: (0, j)),
            pl.BlockSpec((tm, tn), lambda i, j: (i, j)),
        ],
        out_specs=pl.BlockSpec((tm, tn), lambda i, j: (i, j)),
        out_shape=jax.ShapeDtypeStruct((m, n), F32),
        compiler_params=_cparams(("parallel", "arbitrary")),
        name="matmul_residual",
    )(x, w, res)


def _ple_kernel(x_ref, g_ref, wg_ref, p_ref, wp_ref, r_ref, o_ref, xn_ref):
    @pl.when(pl.program_id(1) == 0)
    def _():
        x = x_ref[...]
        ms = jnp.mean(x * x, axis=-1, keepdims=True)
        xn_ref[...] = (x * lax.rsqrt(ms + EPS) * g_ref[...]).astype(BF16)

    gate = _sigmoid(_dot(xn_ref[...], wg_ref[...]))
    proj = _dot(p_ref[...].astype(BF16), wp_ref[...])
    o_ref[...] = r_ref[...] + gate * proj


def ple_update(h, g, wg, p, wp, *, tm=1024, tn=1024):
    m, k = h.shape
    n = wg.shape[1]
    kp = p.shape[1]
    tm, tn = _tile(m, tm), _tile(n, tn)
    return pl.pallas_call(
        _ple_kernel,
        grid=(m // tm, n // tn),
        in_specs=[
            pl.BlockSpec((tm, k), lambda i, j: (i, 0)),
            pl.BlockSpec((1, k), lambda i, j: (0, 0)),
            pl.BlockSpec((k, tn), lambda i, j: (0, j)),
            pl.BlockSpec((tm, kp), lambda i, j: (i, 0)),
            pl.BlockSpec((kp, tn), lambda i, j: (0, j)),
            pl.BlockSpec((tm, tn), lambda i, j: (i, j)),
        ],
        out_specs=pl.BlockSpec((tm, tn), lambda i, j: (i, j)),
        out_shape=jax.ShapeDtypeStruct((m, n), F32),
        scratch_shapes=[pltpu.VMEM((tm, k), BF16)],
        compiler_params=_cparams(("parallel", "arbitrary")),
        name="ple_update",
    )(h, g.reshape(1, k), wg, p, wp, h)


def _mlp_kernel(x_ref, g_ref, w1_ref, w2_ref, o_ref, xn_ref):
    @pl.when(pl.program_id(1) == 0)
    def _():
        x = x_ref[...]
        ms = jnp.mean(x * x, axis=-1, keepdims=True)
        xn_ref[...] = (x * lax.rsqrt(ms + EPS) * g_ref[...]).astype(BF16)
        o_ref[...] = x

    a = jnp.maximum(_dot(xn_ref[...], w1_ref[...]), 0.0)
    o_ref[...] += _dot((a * a).astype(BF16), w2_ref[...])


def mlp_block(h, g, w1, w2, *, tm=1024, tf=512):
    m, k = h.shape
    f = w1.shape[1]
    tm, tf = _tile(m, tm), _tile(f, tf)
    return pl.pallas_call(
        _mlp_kernel,
        grid=(m // tm, f // tf),
        in_specs=[
            pl.BlockSpec((tm, k), lambda i, j: (i, 0)),
            pl.BlockSpec((1, k), lambda i, j: (0, 0)),
            pl.BlockSpec((k, tf), lambda i, j: (0, j)),
            pl.BlockSpec((tf, k), lambda i, j: (j, 0)),
        ],
        out_specs=pl.BlockSpec((tm, k), lambda i, j: (i, 0)),
        out_shape=jax.ShapeDtypeStruct((m, k), F32),
        scratch_shapes=[pltpu.VMEM((tm, k), BF16)],
        compiler_params=_cparams(("parallel", "arbitrary")),
        name="mlp_block",
    )(h, g.reshape(1, k), w1, w2)


def _pad_cols(w, n):
    return jnp.pad(w, ((0, 0), (0, n - w.shape[1])))


def _iota2(shape, dim):
    return lax.broadcasted_iota(jnp.int32, shape, dim)


def _col_to_row(col, eye):
    return jnp.sum(jnp.where(eye, col, 0.0), axis=0, keepdims=True)


def _log_sigmoid(x):
    return jnp.minimum(x, 0.0) - jnp.log(1.0 + jnp.exp(-jnp.abs(x)))


def _run_interleaved(gens, start=None):
    start = start or (0,) * len(gens)
    done = [False] * len(gens)
    tick = 0
    while not all(done):
        for i, g in enumerate(gens):
            if tick >= start[i] and not done[i]:
                try:
                    next(g)
                except StopIteration:
                    done[i] = True
        tick += 1


ML_HEADS = 4
ML_CHUNK = 64


def _mlstm_kernel(q_ref, k_ref, v_ref, og_ref, gt_ref, gb_ref, ng_ref, o_ref, c_ref, n_ref, m_ref, *, dq, dv):
    L = ML_CHUNK

    @pl.when(pl.program_id(1) == 0)
    def _():
        c_ref[...] = jnp.zeros_like(c_ref)
        n_ref[...] = jnp.zeros_like(n_ref)
        m_ref[...] = jnp.zeros_like(m_ref)

    row = _iota2((L, L), 0)
    col = _iota2((L, L), 1)
    incl = row >= col
    eye = row == col
    gates = gt_ref[...] + gb_ref[...]
    scale = dq ** -0.5
    def head_stages(h):
        q = q_ref[:, h * dq:(h + 1) * dq].astype(F32) * scale
        k = k_ref[:, h * dq:(h + 1) * dq].astype(F32)
        v = v_ref[:, h * dv:(h + 1) * dv]
        li_c = gates[:, h:h + 1]
        lf_c = _log_sigmoid(gates[:, ML_HEADS + h:ML_HEADS + h + 1])
        yield
        li_r = _col_to_row(li_c, eye)
        lf_r = _col_to_row(lf_c, eye)
        yield
        b_c = jnp.sum(jnp.where(incl, lf_r, 0.0), axis=1, keepdims=True)
        yield
        b_r = _col_to_row(b_c, eye)
        m_prev = m_ref[h]
        yield
        dlog = jnp.where(incl, b_c - b_r + li_r, NEG)
        inter = b_c + m_prev
        m_t = jnp.maximum(inter, jnp.max(dlog, axis=1, keepdims=True))
        w_inter = jnp.exp(inter - m_t)
        qb = q.astype(BF16)
        kb = k.astype(BF16)
        vb = v.astype(BF16)
        yield
        s = _dot_nt(qb, kb) * jnp.where(incl, jnp.exp(dlog - m_t), 0.0)
        c_prev = c_ref[h]
        yield
        num = _dot(s.astype(BF16), vb) + w_inter * _dot(qb, c_prev.astype(BF16))
        qn = jnp.sum(q * n_ref[h], axis=1, keepdims=True)
        den = jnp.sum(s, axis=1, keepdims=True) + w_inter * qn
        yield
        hc = num / jnp.maximum(jnp.abs(den), jnp.exp(-m_t))
        tot = b_c[L - 1:L, :]
        ulog_c = tot - b_c + li_c
        m_new = jnp.maximum(tot + m_prev, jnp.max(ulog_c, axis=0, keepdims=True))
        wu_c = jnp.exp(ulog_c - m_new)
        dec = jnp.exp(tot + m_prev - m_new)
        kw = k * wu_c
        yield
        c_ref[h] = dec * c_prev + _dot_tn(kw.astype(BF16), vb)
        n_ref[h] = dec * n_ref[h] + jnp.sum(kw, axis=0, keepdims=True)
        m_ref[h] = m_new
        yield
        ms = jnp.mean(hc * hc, axis=1, keepdims=True)
        hn = hc * lax.rsqrt(ms + EPS) * ng_ref[:, h * dv:(h + 1) * dv]
        og = og_ref[:, h * dv:(h + 1) * dv].astype(F32)
        o_ref[:, h * dv:(h + 1) * dv] = (_sigmoid(og) * hn).astype(o_ref.dtype)

    _run_interleaved([head_stages(h) for h in range(ML_HEADS)])


def mlstm_core(proj, gates, gate_b, norm_g, bsz, seq):
    H, L = ML_HEADS, ML_CHUNK
    t = proj.shape[0]
    dq = proj.shape[1] // (6 * H)
    dv = 2 * dq
    nc = seq // L
    gb = jnp.pad(gate_b, (0, LANES - gate_b.shape[0])).reshape(1, LANES)
    rowmap = lambda b, c: b * nc + c
    return pl.pallas_call(
        functools.partial(_mlstm_kernel, dq=dq, dv=dv),
        grid=(bsz, nc),
        in_specs=[
            pl.BlockSpec((L, H * dq), lambda b, c: (rowmap(b, c), 0)),
            pl.BlockSpec((L, H * dq), lambda b, c: (rowmap(b, c), 1)),
            pl.BlockSpec((L, H * dv), lambda b, c: (rowmap(b, c), 1)),
            pl.BlockSpec((L, H * dv), lambda b, c: (rowmap(b, c), 2)),
            pl.BlockSpec((L, LANES), lambda b, c: (rowmap(b, c), 0)),
            pl.BlockSpec((1, LANES), lambda b, c: (0, 0)),
            pl.BlockSpec((1, H * dv), lambda b, c: (0, 0)),
        ],
        out_specs=pl.BlockSpec((L, H * dv), lambda b, c: (rowmap(b, c), 0)),
        out_shape=jax.ShapeDtypeStruct((t, H * dv), BF16),
        scratch_shapes=[
            pltpu.VMEM((H, dq, dv), F32),
            pltpu.VMEM((H, 1, dq), F32),
            pltpu.VMEM((H, 1, 1), F32),
        ],
        compiler_params=_cparams(("parallel", "arbitrary")),
        name="mlstm_core",
    )(proj, proj, proj, proj, gates, gb, norm_g.reshape(1, H * dv))


def mlstm_layer(h, g_mix, w_in, gate_b, norm_g, w_out, bsz, seq):
    nmain = w_in.shape[1] - 2 * ML_HEADS
    proj = norm_matmul(h, g_mix, w_in[:, :nmain].astype(BF16), out_dtype=BF16)
    gates = norm_matmul(h, g_mix, _pad_cols(w_in[:, nmain:], LANES).astype(BF16))
    y = mlstm_core(proj, gates, gate_b, norm_g, bsz, seq)
    return matmul_residual(y, w_out.astype(BF16), h)


M2_CHUNK = 128
M2_GROUPS = 8
M2_STATE = 128
M2_HEAD_DIM = 64
CONV_K = 4
CARRY_ROWS = 8


def _causal_conv_silu(x_raw, carry_ref, w, b, cols=slice(None)):
    n = x_raw.shape[0]
    x_raw = x_raw.astype(F32)
    xe = jnp.concatenate([carry_ref[:, cols], x_raw], axis=0)
    acc = x_raw * w[CONV_K - 1:CONV_K, :]
    if b is not None:
        acc = acc + b
    for k in range(CONV_K - 1):
        sh = CONV_K - 1 - k
        acc = acc + pltpu.roll(xe, sh, axis=0)[CARRY_ROWS:, :] * w[k:k + 1, :]
    carry_ref[:, cols] = x_raw[n - CARRY_ROWS:, :]
    return _silu(acc)


def _ssd_kernel(z_ref, xs_ref, bm_ref, cm_ref, dtc_ref, dtr_ref, cwx_ref, cwb_ref, cwc_ref, cbx_ref, cbb_ref, cbc_ref,
                dtbc_ref, dtbr_ref, alogc_ref, alogr_ref, dsk_ref, ng_ref, o_ref, st_ref, cx_ref, cbm_ref, ccm_ref):
    L, P = M2_CHUNK, M2_HEAD_DIM
    PW = 2 * P
    npair = st_ref.shape[1] // PW
    assert L == PW == st_ref.shape[0]

    @pl.when(pl.program_id(2) == 0)
    def _():
        st_ref[...] = jnp.zeros_like(st_ref)
        cx_ref[...] = jnp.zeros_like(cx_ref)
        cbm_ref[...] = jnp.zeros_like(cbm_ref)
        ccm_ref[...] = jnp.zeros_like(ccm_ref)

    xs = _causal_conv_silu(xs_ref[...], cx_ref, cwx_ref[...], cbx_ref[...])
    bm = _causal_conv_silu(bm_ref[...], cbm_ref, cwb_ref[...], cbb_ref[...])
    cm = _causal_conv_silu(cm_ref[...], ccm_ref, cwc_ref[...], cbc_ref[...])
    dt_c = _softplus(dtc_ref[0, 0] + dtbc_ref[0])
    dt_r = _softplus(dtr_ref[0, 0] + dtbr_ref[0])
    lac_c = dt_c * (-jnp.exp(alogc_ref[0]))
    lac_r = dt_r * (-jnp.exp(alogr_ref[0]))
    row = _iota2((L, L), 0)
    col = _iota2((L, L), 1)
    tri = row >= col
    cum_c = _dot(tri.astype(F32), lac_c, precision=HIGHEST)
    cum_r = _dot(lac_r, (row <= col).astype(F32), precision=HIGHEST)
    cl_c = cum_c[L - 1:L, :]
    cl_r = cum_r[:, L - 1:L]
    cb = _dot_nt(cm.astype(BF16), bm.astype(BF16))
    bmt = bm.T
    st_all = st_ref[...]
    yi = _dot(cm.astype(BF16), st_all.astype(BF16))
    first = _iota2((1, PW), 1) < P

    prs = range(npair)
    colb = [[jnp.broadcast_to(cum_c[:, h:h + 1], (L, L)) for h in (2 * j, 2 * j + 1)] for j in prs]
    rowv = [[cum_r[h:h + 1, :] for h in (2 * j, 2 * j + 1)] for j in prs]
    dtv = [[dt_r[h:h + 1, :] for h in (2 * j, 2 * j + 1)] for j in prs]
    dec = [[jnp.exp(jnp.where(tri, colb[j][i] - rowv[j][i], NEG)) for i in range(2)] for j in prs]
    mm = [[cb * dec[j][i] * dtv[j][i] for i in range(2)] for j in prs]
    ww = [[bmt * (dtv[j][i] * jnp.exp(cl_r[2 * j + i:2 * j + i + 1, :] - rowv[j][i])) for i in range(2)] for j in prs]
    lhs = [jnp.concatenate([jnp.concatenate(mm[j], axis=1), jnp.concatenate(ww[j], axis=1)], axis=0).astype(BF16)
           for j in prs]
    xp = [xs[:, j * PW:(j + 1) * PW] for j in prs]
    rhs = [jnp.concatenate([jnp.where(first, xp[j], 0.0), jnp.where(first, 0.0, xp[j])], axis=0).astype(BF16)
           for j in prs]
    out = [_dot(lhs[j], rhs[j]) for j in prs]
    esc = [jnp.where(first, jnp.exp(colb[j][0]), jnp.exp(colb[j][1])) for j in prs]
    sdec = [jnp.where(first, jnp.exp(cl_c[:, 2 * j:2 * j + 1]), jnp.exp(cl_c[:, 2 * j + 1:2 * j + 2])) for j in prs]
    y = jnp.concatenate([out[j][:L] + yi[:, j * PW:(j + 1) * PW] * esc[j] for j in prs], axis=1)
    st_ref[...] = jnp.concatenate([st_all[:, j * PW:(j + 1) * PW] * sdec[j] + out[j][L:] for j in prs], axis=1)
    y = y + dsk_ref[...] * xs
    y = y * _silu(z_ref[...].astype(F32))
    ms = jnp.mean(y * y, axis=1, keepdims=True)
    o_ref[...] = (y * lax.rsqrt(ms + EPS) * ng_ref[...]).astype(o_ref.dtype)


def ssd_core(proj, dt_raw, conv_w, conv_b, dt_bias, a_log, d_skip, norm_g, bsz, seq):
    L, G, N, P = M2_CHUNK, M2_GROUPS, M2_STATE, M2_HEAD_DIM
    t = proj.shape[0]
    heads = dt_raw.shape[1]
    R = heads // G
    di = heads * P
    gw = di // G
    nc = seq // L
    dtc = dt_raw.reshape(bsz, seq, G, R).transpose(0, 2, 1, 3)
    dtr = dt_raw.reshape(bsz, seq, G, R).transpose(0, 2, 3, 1)
    cw = conv_w
    cbias = conv_b.reshape(1, -1)
    row = lambda b, g, c: b * nc + c
    nzx = di // gw
    nb0 = 2 * di // N
    cwb0 = di // N
    return pl.pallas_call(
        _ssd_kernel,
        grid=(bsz, G, nc),
        in_specs=[
            pl.BlockSpec((L, gw), lambda b, g, c: (row(b, g, c), g)),
            pl.BlockSpec((L, gw), lambda b, g, c: (row(b, g, c), nzx + g)),
            pl.BlockSpec((L, N), lambda b, g, c: (row(b, g, c), nb0 + g)),
            pl.BlockSpec((L, N), lambda b, g, c: (row(b, g, c), nb0 + G + g)),
            pl.BlockSpec((1, 1, L, R), lambda b, g, c: (b, g, c, 0)),
            pl.BlockSpec((1, 1, R, L), lambda b, g, c: (b, g, 0, c)),
            pl.BlockSpec((CONV_K, gw), lambda b, g, c: (0, g)),
            pl.BlockSpec((CONV_K, N), lambda b, g, c: (0, cwb0 + g)),
            pl.BlockSpec((CONV_K, N), lambda b, g, c: (0, cwb0 + G + g)),
            pl.BlockSpec((1, gw), lambda b, g, c: (0, g)),
            pl.BlockSpec((1, N), lambda b, g, c: (0, cwb0 + g)),
            pl.BlockSpec((1, N), lambda b, g, c: (0, cwb0 + G + g)),
            pl.BlockSpec((1, 1, R), lambda b, g, c: (g, 0, 0)),
            pl.BlockSpec((1, R, 1), lambda b, g, c: (g, 0, 0)),
            pl.BlockSpec((1, 1, R), lambda b, g, c: (g, 0, 0)),
            pl.BlockSpec((1, R, 1), lambda b, g, c: (g, 0, 0)),
            pl.BlockSpec((1, gw), lambda b, g, c: (0, g)),
            pl.BlockSpec((1, gw), lambda b, g, c: (0, g)),
        ],
        out_specs=pl.BlockSpec((L, gw), lambda b, g, c: (row(b, g, c), g)),
        out_shape=jax.ShapeDtypeStruct((t, di), BF16),
        scratch_shapes=[
            pltpu.VMEM((N, gw), F32),
            pltpu.VMEM((CARRY_ROWS, gw), F32),
            pltpu.VMEM((CARRY_ROWS, N), F32),
            pltpu.VMEM((CARRY_ROWS, N), F32),
        ],
        compiler_params=_cparams(("parallel", "parallel", "arbitrary")),
        name="ssd_core",
    )(proj, proj, proj, proj, dtc, dtr, cw, cw, cw, cbias, cbias, cbias,
      dt_bias.reshape(G, 1, R), dt_bias.reshape(G, R, 1), a_log.reshape(G, 1, R), a_log.reshape(G, R, 1),
      jnp.repeat(d_skip, P).reshape(1, di), norm_g.reshape(1, di))


def mamba2_layer(h, g_mix, w_in, conv_w, conv_b, dt_bias, a_log, d_skip, norm_g, w_out, bsz, seq):
    heads = dt_bias.shape[0]
    nmain = w_in.shape[1] - heads
    proj = norm_matmul(h, g_mix, w_in[:, :nmain].astype(BF16), out_dtype=BF16)
    dt_raw = norm_matmul(h, g_mix, _pad_cols(w_in[:, nmain:], LANES).astype(BF16))[:, :heads]
    y = ssd_core(proj, dt_raw, conv_w, conv_b, dt_bias, a_log, d_skip, norm_g, bsz, seq)
    return matmul_residual(y, w_out.astype(BF16), h)


GDN_CHUNK = 64
GDN_DH = 128
GDN_STACK = 4
GDN_STACKS = 8
GDN_STAGGER = (0, 0, 0, 0, 8, 8, 8, 8)


def _l2norm(x):
    return x * lax.rsqrt(jnp.sum(x * x, axis=1, keepdims=True) + EPS)


def _gdn_kernel(q_ref, k_ref, v_ref, z_ref, ba_ref, cwq_ref, cwk_ref, cwv_ref, alog_ref, dtb_ref, ng_ref,
                o_ref, st_ref, cq_ref, ck_ref, cv_ref):
    L, DH, HS = GDN_CHUNK, GDN_DH, GDN_STACK
    nv = GDN_STACK * GDN_STACKS
    SL = HS * L

    @pl.when(pl.program_id(2) == 0)
    def _():
        st_ref[...] = jnp.zeros_like(st_ref)
        cq_ref[...] = jnp.zeros_like(cq_ref)
        ck_ref[...] = jnp.zeros_like(ck_ref)
        cv_ref[...] = jnp.zeros_like(cv_ref)

    ba = ba_ref[0, 0]
    beta = _sigmoid(ba[:, :nv])
    gg = -jnp.exp(alog_ref[0]) * _softplus(ba[:, nv:2 * nv] + dtb_ref[0])

    ri = _iota2((SL, SL), 0)
    ci = _iota2((SL, SL), 1)
    same = (ri // L) == (ci // L)
    incl = same & (ri >= ci)
    strict = same & (ri > ci)
    eye = ri == ci
    last = same & ((ci % L) == L - 1)

    heads = range(HS)
    rows = [slice(i * L, (i + 1) * L) for i in heads]
    head_of_row = _iota2((SL, 1), 0) // L

    def stack_stages(s):
        hv0 = s * HS
        kc = slice((hv0 // 2) * DH, (hv0 // 2 + HS // 2) * DH)
        vc = slice(hv0 * DH, (hv0 + HS) * DH)
        q = _causal_conv_silu(q_ref[:, kc], cq_ref, cwq_ref[:, kc], None, kc)
        k = _causal_conv_silu(k_ref[:, kc], ck_ref, cwk_ref[:, kc], None, kc)
        yield
        v = _causal_conv_silu(v_ref[:, vc], cv_ref, cwv_ref[:, vc], None, vc)
        yield
        qn = [_l2norm(q[:, i * DH:(i + 1) * DH]) * DH ** -0.5 for i in range(HS // 2)]
        kn = [_l2norm(k[:, i * DH:(i + 1) * DH]) for i in range(HS // 2)]
        kst = jnp.concatenate([kn[i // 2] for i in heads], axis=0)
        qst = jnp.concatenate([qn[i // 2] for i in heads], axis=0)
        vst = jnp.concatenate([v[:, i * DH:(i + 1) * DH] for i in heads], axis=0)
        beta_c = jnp.concatenate([beta[:, hv0 + i:hv0 + i + 1] for i in heads], axis=0)
        g_c = jnp.concatenate([gg[:, hv0 + i:hv0 + i + 1] for i in heads], axis=0)
        yield
        g_r = _col_to_row(g_c, eye)
        yield
        cum_c = jnp.sum(jnp.where(incl, g_r, 0.0), axis=1, keepdims=True)
        yield
        cum_r = _col_to_row(cum_c, eye)
        yield
        dmask = jnp.exp(jnp.where(incl, cum_c - cum_r, NEG))
        kb = kst * beta_c
        kst_b = kst.astype(BF16)
        yield
        nmat = jnp.where(strict, _dot_nt(kb.astype(BF16), kst_b) * dmask, 0.0)
        ecum = jnp.exp(cum_c)
        rhs = jnp.concatenate([vst * beta_c, kb * ecum], axis=1)
        yield
        pw_b = nmat.astype(BF16)
        sol = rhs - _dot(pw_b, rhs.astype(BF16))
        yield
        for _ in range(5):
            pw_b = _dot(pw_b, pw_b).astype(BF16)
            sol = sol + _dot(pw_b, sol.astype(BF16))
            yield
        qe = qst * ecum
        attn = _dot_nt(qst.astype(BF16), kst_b) * dmask
        cl_c = jnp.sum(jnp.where(last, cum_r, 0.0), axis=1, keepdims=True)
        kd = kst * jnp.exp(cl_c - cum_c)
        st_all = st_ref[s]
        yield
        both = [_dot(jnp.concatenate([sol[rows[i], DH:], qe[rows[i]]], axis=0).astype(BF16),
                     st_all[:, i * DH:(i + 1) * DH].astype(BF16)) for i in heads]
        yield
        vnew = jnp.concatenate([sol[rows[i], :DH] - both[i][:L] for i in heads], axis=0)
        o = jnp.concatenate([both[i][L:] for i in heads], axis=0) + _dot(attn.astype(BF16), vnew.astype(BF16))
        yield
        vbd = jnp.concatenate([jnp.where(head_of_row == i, vnew, 0.0) for i in heads], axis=1)
        decay = jnp.concatenate(
            [jnp.broadcast_to(jnp.exp(cl_c[i * L:i * L + 1, :]), (1, DH)) for i in heads], axis=1)
        st_ref[s] = st_all * decay + _dot(kd.T.astype(BF16), vbd.astype(BF16))
        yield
        for i in heads:
            hv = hv0 + i
            oi = o[rows[i]]
            ms = jnp.mean(oi * oi, axis=1, keepdims=True)
            on = oi * lax.rsqrt(ms + EPS) * ng_ref[...]
            zi = z_ref[:, hv * DH:(hv + 1) * DH].astype(F32)
            o_ref[:, hv * DH:(hv + 1) * DH] = (on * _silu(zi)).astype(o_ref.dtype)

    _run_interleaved([stack_stages(s) for s in range(GDN_STACKS)], GDN_STAGGER)


def gdn_core(proj, ba_raw, conv_w, a_log, dt_bias, norm_g, bsz, seq):
    L, DH = GDN_CHUNK, GDN_DH
    nv = GDN_STACK * GDN_STACKS
    t = proj.shape[0]
    hv_total = ba_raw.shape[1] // 2
    ng = hv_total // nv
    qw = (nv // 2) * DH
    vw = nv * DH
    nc = seq // L
    b_part = ba_raw[:, :hv_total].reshape(bsz, seq, ng, nv)
    a_part = ba_raw[:, hv_total:].reshape(bsz, seq, ng, nv)
    ba = jnp.concatenate([b_part, a_part], axis=-1).transpose(0, 2, 1, 3)
    row = lambda b, g, c: b * nc + c
    kq0 = ng
    v0 = 2 * ng * qw // vw
    return pl.pallas_call(
        _gdn_kernel,
        grid=(bsz, ng, nc),
        in_specs=[
            pl.BlockSpec((L, qw), lambda b, g, c: (row(b, g, c), g)),
            pl.BlockSpec((L, qw), lambda b, g, c: (row(b, g, c), kq0 + g)),
            pl.BlockSpec((L, vw), lambda b, g, c: (row(b, g, c), v0 + g)),
            pl.BlockSpec((L, vw), lambda b, g, c: (row(b, g, c), v0 + ng + g)),
            pl.BlockSpec((1, 1, L, 2 * nv), lambda b, g, c: (b, g, c, 0)),
            pl.BlockSpec((CONV_K, qw), lambda b, g, c: (0, g)),
            pl.BlockSpec((CONV_K, qw), lambda b, g, c: (0, kq0 + g)),
            pl.BlockSpec((CONV_K, vw), lambda b, g, c: (0, v0 + g)),
            pl.BlockSpec((1, 1, nv), lambda b, g, c: (g, 0, 0)),
            pl.BlockSpec((1, 1, nv), lambda b, g, c: (g, 0, 0)),
            pl.BlockSpec((1, DH), lambda b, g, c: (0, 0)),
        ],
        out_specs=pl.BlockSpec((L, vw), lambda b, g, c: (row(b, g, c), g)),
        out_shape=jax.ShapeDtypeStruct((t, hv_total * DH), BF16),
        scratch_shapes=[
            pltpu.VMEM((GDN_STACKS, DH, GDN_STACK * DH), F32),
            pltpu.VMEM((CARRY_ROWS, qw), F32),
            pltpu.VMEM((CARRY_ROWS, qw), F32),
            pltpu.VMEM((CARRY_ROWS, vw), F32),
        ],
        compiler_params=_cparams(("parallel", "parallel", "arbitrary")),
        name="gdn_core",
    )(proj, proj, proj, proj, ba, conv_w, conv_w, conv_w,
      a_log.reshape(ng, 1, nv), dt_bias.reshape(ng, 1, nv), norm_g.reshape(1, DH))


def gdn_layer(h, g_mix, w_in, conv_w, a_log, dt_bias, norm_g, w_out, bsz, seq):
    hv = a_log.shape[0]
    nmain = w_in.shape[1] - 2 * hv
    proj = norm_matmul(h, g_mix, w_in[:, :nmain].astype(BF16), out_dtype=BF16)
    ba_raw = norm_matmul(h, g_mix, _pad_cols(w_in[:, nmain:], LANES).astype(BF16))[:, :2 * hv]
    y = gdn_core(proj, ba_raw, conv_w, a_log, dt_bias, norm_g, bsz, seq)
    return matmul_residual(y, w_out.astype(BF16), h)


NSA_DK = 128
NSA_G = 4
NSA_HPG = 4
NSA_STRIDE = 16
NSA_CMP_BLOCK = 32
NSA_SEL = 64
NSA_TOP_N = 16
NSA_WINDOW = 512
NSA_TQ = 128
NSA_TKS = 512
NSA_TKW = 128
FORCE_BONUS = 1e4


def _cmp_kernel(x_ref, pa_ref, pb_ref, wa_ref, wb_ref, w2_ref, kg_ref, o_ref, acca_ref, accb_ref):
    kk = pl.program_id(2)

    @pl.when(kk == 0)
    def _():
        acca_ref[...] = jnp.zeros_like(acca_ref)
        accb_ref[...] = jnp.zeros_like(accb_ref)

    x = x_ref[0, 0].astype(F32)
    acca_ref[...] += _dot((x + pa_ref[0]).astype(BF16), wa_ref[0])
    accb_ref[...] += _dot((x + pb_ref[0]).astype(BF16), wb_ref[0])

    @pl.when(kk == pl.num_programs(2) - 1)
    def _():
        nrow = acca_ref.shape[0]
        pre = acca_ref[...] + pltpu.roll(accb_ref[...], nrow - 1, axis=0)
        y = _dot(_silu(pre).astype(BF16), w2_ref[0])
        is_k = pl.program_id(0) == 0
        for g in range(NSA_G):
            yg = y[:, g * NSA_DK:(g + 1) * NSA_DK]
            ms = jnp.mean(yg * yg, axis=1, keepdims=True)
            yn = yg * lax.rsqrt(ms + EPS) * kg_ref[...]
            o_ref[0, 0, :, g * NSA_DK:(g + 1) * NSA_DK] = jnp.where(is_k, yn, yg)


def nsa_compress(xkv, pos, w1, w2, kg0, *, tk=2048):
    _, bsz, ngrp, kdim = xkv.shape
    G, dk, half = NSA_G, NSA_DK, NSA_CMP_BLOCK // 2
    eye = jnp.eye(G, dtype=w1.dtype)

    def expand_w1(w):
        return jnp.einsum("clde,gh->clgdhe", w, eye).reshape(2, half * G * dk, G * dk).astype(BF16)

    def expand_pos(ps):
        return jnp.broadcast_to(ps[:, :, None, :], (2, half, G, dk)).reshape(2, 1, half * G * dk)

    wa, wb = expand_w1(w1[:, :half]), expand_w1(w1[:, half:])
    pa, pb = expand_pos(pos[:, :half]), expand_pos(pos[:, half:])
    w2bd = jnp.einsum("cde,gh->cgdhe", w2, eye).reshape(2, G * dk, G * dk).astype(BF16)
    tk = min(tk, kdim)
    assert kdim % tk == 0
    return pl.pallas_call(
        _cmp_kernel,
        grid=(2, bsz, kdim // tk),
        in_specs=[
            pl.BlockSpec((1, 1, ngrp, tk), lambda c, b, l: (c, b, 0, l)),
            pl.BlockSpec((1, 1, tk), lambda c, b, l: (c, 0, l)),
            pl.BlockSpec((1, 1, tk), lambda c, b, l: (c, 0, l)),
            pl.BlockSpec((1, tk, G * dk), lambda c, b, l: (c, l, 0)),
            pl.BlockSpec((1, tk, G * dk), lambda c, b, l: (c, l, 0)),
            pl.BlockSpec((1, G * dk, G * dk), lambda c, b, l: (c, 0, 0)),
            pl.BlockSpec((1, dk), lambda c, b, l: (0, 0)),
        ],
        out_specs=pl.BlockSpec((1, 1, ngrp, G * dk), lambda c, b, l: (c, b, 0, 0)),
        out_shape=jax.ShapeDtypeStruct((2, bsz, ngrp, G * dk), F32),
        scratch_shapes=[pltpu.VMEM((ngrp, G * dk), F32), pltpu.VMEM((ngrp, G * dk), F32)],
        compiler_params=_cparams(("parallel", "parallel", "arbitrary")),
        name="nsa_compress",
    )(xkv, pa, pb, wa, wb, w2bd, kg0.reshape(1, dk))


def _nsa_attn_kernel(q_ref, ks_ref, vs_ref, kw_ref, vw_ref, kc_ref, vc_ref, gate_ref, qg_ref, kg_ref, o_ref,
                     ksn_ref, vst_ref, kwn_ref, vwt_ref, kcb_ref, vct_ref, sel_ref, *, n_top):
    TQ, TS, TW, DK, HPG = NSA_TQ, NSA_TKS, NSA_TKW, NSA_DK, NSA_HPG
    i = pl.program_id(2)
    ncmp = kcb_ref.shape[0]
    nsel = sel_ref.shape[0]
    rows = HPG * TQ

    @pl.when(i == 0)
    def _():
        def norm_key(kt, gidx):
            ms = jnp.mean(kt * kt, axis=1, keepdims=True)
            return (kt * lax.rsqrt(ms + EPS) * kg_ref[gidx:gidx + 1, :]).astype(BF16)

        def prep_sel(t, carry):
            sl = pl.ds(pl.multiple_of(t * TS, TS), TS)
            ksn_ref[t] = norm_key(ks_ref[sl, :].astype(F32), 1)
            vst_ref[t] = vs_ref[sl, :].astype(F32).T.astype(BF16)
            return carry

        def prep_win(t, carry):
            sl = pl.ds(pl.multiple_of(t * TW, TW), TW)
            kwn_ref[t] = norm_key(kw_ref[sl, :].astype(F32), 2)
            vwt_ref[t] = vw_ref[sl, :].astype(F32).T.astype(BF16)
            return carry

        lax.fori_loop(0, ksn_ref.shape[0], prep_sel, 0)
        lax.fori_loop(0, kwn_ref.shape[0], prep_win, 0)
        kcb_ref[...] = kc_ref[0, 0].astype(BF16)
        vct_ref[...] = vc_ref[0, 0].T.astype(BF16)

    s0 = i * TQ
    q = q_ref[...].astype(F32)
    scale = DK ** -0.5
    qparts = []
    for r in range(HPG):
        qr = q[:, r * DK:(r + 1) * DK]
        ms = jnp.mean(qr * qr, axis=1, keepdims=True)
        qparts.append(qr * lax.rsqrt(ms + EPS) * (qg_ref[...] * scale))
    qrows = jnp.concatenate(qparts, axis=0).astype(BF16)
    tq = s0 + (_iota2((1, rows), 1) % TQ)

    res = {}

    def compressed_stages():
        sc = _dot_nt(kcb_ref[...], qrows)
        c_end = _iota2((ncmp, 1), 0) * NSA_STRIDE + (NSA_CMP_BLOCK - 1)
        maskc = c_end <= tq
        yield
        sc = jnp.where(maskc, sc, NEG)
        e = jnp.where(maskc, jnp.exp(sc - jnp.max(sc, axis=0, keepdims=True)), 0.0)
        yield
        pc = e / jnp.maximum(jnp.sum(e, axis=0, keepdims=True), 1e-30)
        res["oc"] = _dot(vct_ref[...], pc.astype(BF16))
        yield
        pcs = pc[:, 0:TQ]
        for r in range(1, HPG):
            pcs = pcs + pc[:, r * TQ:(r + 1) * TQ]
        jj = _iota2((nsel, ncmp), 0) * NSA_SEL
        cs = _iota2((nsel, ncmp), 1) * NSA_STRIDE
        overlap = ((cs < jj + NSA_SEL) & (cs + (NSA_CMP_BLOCK - 1) >= jj)).astype(F32)
        imp = _dot(overlap, pcs, precision=HIGHEST)
        yield
        jcol = _iota2((nsel, TQ), 0)
        cur = (s0 + _iota2((nsel, TQ), 1)) // NSA_SEL
        forced = (jcol == 0) | (jcol == cur) | (jcol == cur - 1)
        imp = jnp.where(jcol <= cur, imp + jnp.where(forced, FORCE_BONUS, 0.0), NEG)
        rank = jnp.zeros((nsel, TQ), F32)
        for j in range(nsel):
            vj = imp[j:j + 1, :]
            beats = (vj > imp) | ((vj == imp) & (jcol > j))
            rank = rank + beats.astype(F32)
            if j % 8 == 7:
                yield
        sel_ref[...] = jnp.where(rank < n_top, 0.0, NEG)

    def window_stages():
        nwt = (NSA_WINDOW + TQ) // TW
        kt0 = i - NSA_WINDOW // TW
        idx = [jnp.maximum(kt0 + d, 0) for d in range(nwt)]
        sw = _dot_nt(jnp.concatenate([kwn_ref[ix] for ix in idx], axis=0), qrows)
        yield
        kofs = _iota2((TW, 1), 0)
        parts = []
        for d in range(nwt):
            kt = kt0 + d
            sd = sw[d * TW:(d + 1) * TW]
            if d == 0:
                sd = jnp.where((kt * TW + kofs > tq - NSA_WINDOW) & (kt >= 0), sd, NEG)
            elif d == nwt - 1:
                sd = jnp.where(kt * TW + kofs <= tq, sd, NEG)
            else:
                sd = sd + jnp.where(kt >= 0, 0.0, NEG)
            parts.append(sd)
        sw = jnp.concatenate(parts, axis=0)
        yield
        pw = jnp.exp(sw - jnp.max(sw, axis=0, keepdims=True))
        yield
        res["l_w"] = jnp.sum(pw, axis=0, keepdims=True)
        pwb = pw.astype(BF16)
        acc_w = _dot(vwt_ref[idx[0]], pwb[0:TW])
        for d in range(1, nwt):
            acc_w = acc_w + _dot(vwt_ref[idx[d]], pwb[d * TW:(d + 1) * TW])
        res["acc_w"] = acc_w

    _run_interleaved([compressed_stages(), window_stages()])
    oc, l_w, acc_w = res["oc"], res["l_w"], res["acc_w"]

    bpt = TS // NSA_SEL

    def sel_scores(kt):
        s = _dot_nt(ksn_ref[kt], qrows)
        parts = []
        for jb in range(bpt):
            srow = sel_ref[pl.ds(kt * bpt + jb, 1), :]
            srow = jnp.concatenate([srow] * HPG, axis=1)
            parts.append(jnp.broadcast_to(srow, (NSA_SEL, rows)))
        return s + jnp.concatenate(parts, axis=0)

    def online(s, carry, vt):
        m, l, acc = carry
        m_new = jnp.maximum(m, jnp.max(s, axis=0, keepdims=True))
        alpha = jnp.exp(m - m_new)
        pexp = jnp.exp(s - m_new)
        l = alpha * l + jnp.sum(pexp, axis=0, keepdims=True)
        acc = alpha * acc + _dot(vt, pexp.astype(BF16))
        return m_new, l, acc

    def sel_body(kt, carry):
        return online(sel_scores(kt), carry, vst_ref[kt])

    init = (jnp.full((1, rows), NEG, F32), jnp.zeros((1, rows), F32), jnp.zeros((DK, rows), F32))
    kd = s0 // TS
    carry = lax.fori_loop(0, kd, sel_body, init)
    kpos = kd * TS + _iota2((TS, 1), 0)
    _, l_s, acc_s = online(jnp.where(kpos <= tq, sel_scores(kd), NEG), carry, vst_ref[kd])

    gt = _sigmoid(gate_ref[...])

    def grow(c):
        return jnp.concatenate([gt[c * HPG + r:c * HPG + r + 1, :] for r in range(HPG)], axis=1)

    out = (grow(0) * oc + grow(1) * (acc_s / jnp.maximum(l_s, 1e-30))
           + grow(2) * (acc_w / jnp.maximum(l_w, 1e-30)))
    for r in range(HPG):
        o_ref[:, r * DK:(r + 1) * DK] = out[:, r * TQ:(r + 1) * TQ].T.astype(o_ref.dtype)


def nsa_attention(proj, kvc, gate, q_g, k_g, bsz, seq):
    G, HPG, DK, TQ, TS, TW = NSA_G, NSA_HPG, NSA_DK, NSA_TQ, NSA_TKS, NSA_TKW
    assert TQ == TW and seq % TS == 0
    t = proj.shape[0]
    nq = seq // TQ
    ncmp = kvc.shape[2]
    nsel = seq // NSA_SEL
    n_top = min(NSA_TOP_N, nsel)
    qw = HPG * DK
    kv0 = G * HPG

    def kvspec(which):
        return pl.BlockSpec((seq, DK), lambda b, g, i: (b, kv0 + which * G + g))

    return pl.pallas_call(
        functools.partial(_nsa_attn_kernel, n_top=n_top),
        grid=(bsz, G, nq),
        in_specs=[
            pl.BlockSpec((TQ, qw), lambda b, g, i: (b * nq + i, g)),
            kvspec(2), kvspec(3), kvspec(4), kvspec(5),
            pl.BlockSpec((1, 1, ncmp, DK), lambda b, g, i: (0, b, 0, g)),
            pl.BlockSpec((1, 1, ncmp, DK), lambda b, g, i: (1, b, 0, g)),
            pl.BlockSpec((4 * HPG, TQ), lambda b, g, i: (g, b * nq + i)),
            pl.BlockSpec((1, DK), lambda b, g, i: (0, 0)),
            pl.BlockSpec((3, DK), lambda b, g, i: (0, 0)),
        ],
        out_specs=pl.BlockSpec((TQ, qw), lambda b, g, i: (b * nq + i, g)),
        out_shape=jax.ShapeDtypeStruct((t, G * qw), BF16),
        scratch_shapes=[
            pltpu.VMEM((seq // TS, TS, DK), BF16),
            pltpu.VMEM((seq // TS, DK, TS), BF16),
            pltpu.VMEM((seq // TW, TW, DK), BF16),
            pltpu.VMEM((seq // TW, DK, TW), BF16),
            pltpu.VMEM((ncmp, DK), BF16),
            pltpu.VMEM((DK, ncmp), BF16),
            pltpu.VMEM((nsel, TQ), F32),
        ],
        compiler_params=_cparams(("parallel", "parallel", "arbitrary")),
        name="nsa_attention",
    )(proj, proj, proj, proj, proj, kvc, kvc, gate, q_g.reshape(1, DK), k_g)


def nsa_layer(h, g_mix, w_in, q_g, k_g, cmp_pos, cmp_w1, cmp_w2, w_out, bsz, seq):
    G, HPG, DK = NSA_G, NSA_HPG, NSA_DK
    H = G * HPG
    nmain = H * DK + 6 * G * DK
    proj = norm_matmul(h, g_mix, w_in[:, :nmain].astype(BF16), out_dtype=BF16)
    rows = range(G * 4 * HPG)
    perm = jnp.array([((n // (4 * HPG)) * HPG + n % HPG) * 3 + (n % (4 * HPG)) // HPG
                      if (n % (4 * HPG)) // HPG < 3 else LANES - 1 for n in rows] + [LANES - 1] * (LANES - len(rows)))
    gate_t = norm_matmul_t(h, g_mix, _pad_cols(w_in[:, nmain:], LANES)[:, perm].astype(BF16))
    ngrp = seq // NSA_STRIDE
    kvw = G * DK
    xkv = jnp.stack([
        proj[:, H * DK:H * DK + kvw].reshape(bsz, ngrp, NSA_STRIDE * kvw),
        proj[:, H * DK + kvw:H * DK + 2 * kvw].reshape(bsz, ngrp, NSA_STRIDE * kvw),
    ])
    kvc = nsa_compress(xkv, cmp_pos, cmp_w1, cmp_w2, k_g[0])
    o = nsa_attention(proj, kvc, gate_t, q_g, k_g, bsz, seq)
    return matmul_residual(o, w_out.astype(BF16), h)


def kernel(x, p, norm_mix_g, norm_mlp_g, mlp_w1, mlp_w2, ple_norm_g, ple_w_gate, ple_w_proj, nsa_w_in, nsa_q_g, nsa_k_g, nsa_cmp_pos, nsa_cmp_w1, nsa_cmp_w2, nsa_w_out, m2_w_in, m2_conv_w, m2_conv_b, m2_dt_bias, m2_a_log, m2_d, m2_norm_g, m2_w_out, gdn_w_in, gdn_conv_w, gdn_a_log, gdn_dt_bias, gdn_norm_g, gdn_w_out, ml_w_in, ml_gate_b, ml_norm_g, ml_w_out):
    bsz, seq, d = x.shape
    h = x.reshape(bsz * seq, d)
    for i in range(p.shape[0]):
        kind, j = i % 4, i // 4
        if kind == 0:
            h = nsa_layer(h, norm_mix_g[i], nsa_w_in[j], nsa_q_g[j], nsa_k_g[j], nsa_cmp_pos[j], nsa_cmp_w1[j], nsa_cmp_w2[j], nsa_w_out[j], bsz, seq)
        elif kind == 1:
            h = mamba2_layer(h, norm_mix_g[i], m2_w_in[j], m2_conv_w[j], m2_conv_b[j], m2_dt_bias[j], m2_a_log[j], m2_d[j], m2_norm_g[j], m2_w_out[j], bsz, seq)
        elif kind == 2:
            h = gdn_layer(h, norm_mix_g[i], gdn_w_in[j], gdn_conv_w[j], gdn_a_log[j], gdn_dt_bias[j], gdn_norm_g[j], gdn_w_out[j], bsz, seq)
        elif kind == 3:
            h = mlstm_layer(h, norm_mix_g[i], ml_w_in[j], ml_gate_b[j], ml_norm_g[j], ml_w_out[j], bsz, seq)
        h = mlp_block(h, norm_mlp_g[i], mlp_w1[i].astype(BF16), mlp_w2[i].astype(BF16))
        h = ple_update(h, ple_norm_g[i], ple_w_gate[i].astype(BF16), p[i].reshape(bsz * seq, -1), ple_w_proj[i].astype(BF16))
    return h.reshape(bsz, seq, d)
```

```python
import functools
import math

import jax
import jax.numpy as jnp
from jax import lax
from jax.experimental import pallas as pl
from jax.experimental.pallas import tpu as pltpu

F32 = jnp.float32
BF16 = jnp.bfloat16
EPS = 1e-6
NEG = -1e30
HIGHEST = lax.Precision.HIGHEST

VMEM_LIMIT_BYTES = 56 * 1024 * 1024
LANES = 128


def _cparams(sem):
    return pltpu.CompilerParams(dimension_semantics=sem, vmem_limit_bytes=VMEM_LIMIT_BYTES)


def _silu(x):
    return x * (1.0 / (1.0 + jnp.exp(-x)))


def _sigmoid(x):
    return 1.0 / (1.0 + jnp.exp(-x))


def _softplus(x):
    return jnp.maximum(x, 0.0) + jnp.log(1.0 + jnp.exp(-jnp.abs(x)))


def _dot(a, b, **kw):
    return jnp.dot(a, b, preferred_element_type=F32, **kw)


def _dot_nt(a, b, **kw):
    return lax.dot_general(a, b, (((1,), (1,)), ((), ())), preferred_element_type=F32, **kw)


def _dot_tn(a, b, **kw):
    return lax.dot_general(a, b, (((0,), (0,)), ((), ())), preferred_element_type=F32, **kw)


NORM_CHUNKS = 4


def _norm_row_chunks(x_ref, g_ref, xn_ref, body):
    rc = x_ref.shape[0] // NORM_CHUNKS
    for c in range(NORM_CHUNKS):
        rows = slice(c * rc, (c + 1) * rc)
        x = x_ref[rows, :]
        ms = jnp.mean(x * x, axis=-1, keepdims=True)
        xn = (x * lax.rsqrt(ms + EPS) * g_ref[...]).astype(BF16)
        xn_ref[rows, :] = xn
        body(rows, x, xn)


def _norm_mm_kernel(x_ref, g_ref, w_ref, o_ref, xn_ref):
    first = pl.program_id(1) == 0

    @pl.when(first)
    def _():
        def body(rows, x, xn):
            o_ref[rows, :] = _dot(xn, w_ref[...]).astype(o_ref.dtype)

        _norm_row_chunks(x_ref, g_ref, xn_ref, body)

    @pl.when(jnp.logical_not(first))
    def _():
        o_ref[...] = _dot(xn_ref[...], w_ref[...]).astype(o_ref.dtype)


def _tile(dim, want):
    t = min(want, dim)
    while dim % t:
        t //= 2
    return t


def norm_matmul(x, g, w, *, out_dtype=F32, tm=1024, tn=2048):
    m, k = x.shape
    n = w.shape[1]
    tm, tn = _tile(m, tm), _tile(n, tn)
    return pl.pallas_call(
        _norm_mm_kernel,
        grid=(m // tm, n // tn),
        in_specs=[
            pl.BlockSpec((tm, k), lambda i, j: (i, 0)),
            pl.BlockSpec((1, k), lambda i, j: (0, 0)),
            pl.BlockSpec((k, tn), lambda i, j: (0, j)),
        ],
        out_specs=pl.BlockSpec((tm, tn), lambda i, j: (i, j)),
        out_shape=jax.ShapeDtypeStruct((m, n), out_dtype),
        scratch_shapes=[pltpu.VMEM((tm, k), BF16)],
        compiler_params=_cparams(("parallel", "arbitrary")),
        name="norm_matmul",
    )(x, g.reshape(1, k), w)


def _norm_mm_t_kernel(x_ref, g_ref, w_ref, o_ref):
    x = x_ref[...]
    ms = jnp.mean(x * x, axis=-1, keepdims=True)
    xn = (x * lax.rsqrt(ms + EPS) * g_ref[...]).astype(BF16)
    o_ref[...] = _dot(xn, w_ref[...]).T


def norm_matmul_t(x, g, w, *, tm=1024):
    m, k = x.shape
    n = w.shape[1]
    assert n == LANES
    tm = _tile(m, tm)
    return pl.pallas_call(
        _norm_mm_t_kernel,
        grid=(m // tm,),
        in_specs=[
            pl.BlockSpec((tm, k), lambda i: (i, 0)),
            pl.BlockSpec((1, k), lambda i: (0, 0)),
            pl.BlockSpec((k, n), lambda i: (0, 0)),
        ],
        out_specs=pl.BlockSpec((n, tm), lambda i: (0, i)),
        out_shape=jax.ShapeDtypeStruct((n, m), F32),
        compiler_params=_cparams(("parallel",)),
        name="norm_matmul_t",
    )(x, g.reshape(1, k), w)


def _mm_res_kernel(x_ref, w_ref, r_ref, o_ref):
    o_ref[...] = r_ref[...] + _dot(x_ref[...], w_ref[...])


def matmul_residual(x, w, res, *, tm=1024, tn=1024):
    m, k = x.shape
    n = w.shape[1]
    tm, tn = _tile(m, tm), _tile(n, tn)
    while 2 * (tm * k * 2 + k * tn * 2 + 2 * tm * tn * 4) + tm * tn * 4 > 0.8 * VMEM_LIMIT_BYTES:
        tn //= 2
    return pl.pallas_call(
        _mm_res_kernel,
        grid=(m // tm, n // tn),
        in_specs=[
            pl.BlockSpec((tm, k), lambda i, j: (i, 0)),
            pl.BlockSpec((k, tn), lambda i, j: (0, j)),
            pl.BlockSpec((tm, tn), lambda i, j: (i, j)),
        ],
        out_specs=pl.BlockSpec((tm, tn), lambda i, j: (i, j)),
        out_shape=jax.ShapeDtypeStruct((m, n), F32),
        compiler_params=_cparams(("parallel", "arbitrary")),
        name="matmul_residual",
    )(x, w, res)


def _ple_kernel(x_ref, g_ref, wg_ref, p_ref, wp_ref, r_ref, o_ref, xn_ref):
    first = pl.program_id(1) == 0

    def update(rows, xn):
        gate = _sigmoid(_dot(xn, wg_ref[...]))
        proj = _dot(p_ref[rows, :].astype(BF16), wp_ref[...])
        o_ref[rows, :] = r_ref[rows, :] + gate * proj

    @pl.when(first)
    def _():
        _norm_row_chunks(x_ref, g_ref, xn_ref, lambda rows, x, xn: update(rows, xn))

    @pl.when(jnp.logical_not(first))
    def _():
        update(slice(None), xn_ref[...])


def ple_update(h, g, wg, p, wp, *, tm=1024, tn=1024):
    m, k = h.shape
    n = wg.shape[1]
    kp = p.shape[1]
    tm, tn = _tile(m, tm), _tile(n, tn)
    return pl.pallas_call(
        _ple_kernel,
        grid=(m // tm, n // tn),
        in_specs=[
            pl.BlockSpec((tm, k), lambda i, j: (i, 0)),
            pl.BlockSpec((1, k), lambda i, j: (0, 0)),
            pl.BlockSpec((k, tn), lambda i, j: (0, j)),
            pl.BlockSpec((tm, kp), lambda i, j: (i, 0)),
            pl.BlockSpec((kp, tn), lambda i, j: (0, j)),
            pl.BlockSpec((tm, tn), lambda i, j: (i, j)),
        ],
        out_specs=pl.BlockSpec((tm, tn), lambda i, j: (i, j)),
        out_shape=jax.ShapeDtypeStruct((m, n), F32),
        scratch_shapes=[pltpu.VMEM((tm, k), BF16)],
        compiler_params=_cparams(("parallel", "arbitrary")),
        name="ple_update",
    )(h, g.reshape(1, k), wg, p, wp, h)


def _mlp_kernel(x_ref, g_ref, w1_ref, w2_ref, o_ref, xn_ref):
    first = pl.program_id(1) == 0

    def ffn(xn):
        a = jnp.maximum(_dot(xn, w1_ref[0]), 0.0)
        return _dot((a * a).astype(BF16), w2_ref[...])

    @pl.when(first)
    def _():
        def body(rows, x, xn):
            o_ref[rows, :] = x + ffn(xn)

        _norm_row_chunks(x_ref, g_ref, xn_ref, body)

    @pl.when(jnp.logical_not(first))
    def _():
        o_ref[...] += ffn(xn_ref[...])


def mlp_block(h, g, w1, w2, *, tm=1024, tf=512):
    m, k = h.shape
    f = w1.shape[1]
    tm, tf = _tile(m, tm), _tile(f, tf)
    w1t = w1.reshape(k, f // tf, tf).transpose(1, 0, 2)
    return pl.pallas_call(
        _mlp_kernel,
        grid=(m // tm, f // tf),
        in_specs=[
            pl.BlockSpec((tm, k), lambda i, j: (i, 0)),
            pl.BlockSpec((1, k), lambda i, j: (0, 0)),
            pl.BlockSpec((1, k, tf), lambda i, j: (j, 0, 0)),
            pl.BlockSpec((tf, k), lambda i, j: (j, 0)),
        ],
        out_specs=pl.BlockSpec((tm, k), lambda i, j: (i, 0)),
        out_shape=jax.ShapeDtypeStruct((m, k), F32),
        scratch_shapes=[pltpu.VMEM((tm, k), BF16)],
        compiler_params=_cparams(("parallel", "arbitrary")),
        name="mlp_block",
    )(h, g.reshape(1, k), w1t, w2)


def _pad_cols(w, n):
    return jnp.pad(w, ((0, 0), (0, n - w.shape[1])))


def _iota2(shape, dim):
    return lax.broadcasted_iota(jnp.int32, shape, dim)


def _col_to_row(col, eye):
    return jnp.sum(jnp.where(eye, col, 0.0), axis=0, keepdims=True)


def _log_sigmoid(x):
    return jnp.minimum(x, 0.0) - jnp.log(1.0 + jnp.exp(-jnp.abs(x)))


def _run_interleaved(gens, start=None):
    start = start or (0,) * len(gens)
    done = [False] * len(gens)
    tick = 0
    while not all(done):
        for i, g in enumerate(gens):
            if tick >= start[i] and not done[i]:
                try:
                    next(g)
                except StopIteration:
                    done[i] = True
        tick += 1


ML_HEADS = 4
ML_CHUNK = 64


def _mlstm_kernel(q_ref, k_ref, v_ref, og_ref, gt_ref, gb_ref, ng_ref, o_ref, c_ref, n_ref, m_ref, *, dq, dv):
    L = ML_CHUNK

    @pl.when(pl.program_id(1) == 0)
    def _():
        c_ref[...] = jnp.zeros_like(c_ref)
        n_ref[...] = jnp.zeros_like(n_ref)
        m_ref[...] = jnp.zeros_like(m_ref)

    row = _iota2((L, L), 0)
    col = _iota2((L, L), 1)
    incl = row >= col
    eye = row == col
    gates = gt_ref[...] + gb_ref[...]
    scale = dq ** -0.5
    def head_stages(h):
        q = q_ref[:, h * dq:(h + 1) * dq].astype(F32) * scale
        k = k_ref[:, h * dq:(h + 1) * dq].astype(F32)
        v = v_ref[:, h * dv:(h + 1) * dv]
        li_c = gates[:, h:h + 1]
        lf_c = _log_sigmoid(gates[:, ML_HEADS + h:ML_HEADS + h + 1])
        yield
        li_r = _col_to_row(li_c, eye)
        lf_r = _col_to_row(lf_c, eye)
        yield
        b_c = jnp.sum(jnp.where(incl, lf_r, 0.0), axis=1, keepdims=True)
        yield
        b_r = _col_to_row(b_c, eye)
        m_prev = m_ref[h]
        yield
        dlog = jnp.where(incl, b_c - b_r + li_r, NEG)
        inter = b_c + m_prev
        m_t = jnp.maximum(inter, jnp.max(dlog, axis=1, keepdims=True))
        w_inter = jnp.exp(inter - m_t)
        qb = q.astype(BF16)
        kb = k.astype(BF16)
        vb = v.astype(BF16)
        yield
        s = _dot_nt(qb, kb) * jnp.where(incl, jnp.exp(dlog - m_t), 0.0)
        c_prev = c_ref[h]
        yield
        num = _dot(s.astype(BF16), vb) + w_inter * _dot(qb, c_prev.astype(BF16))
        qn = jnp.sum(q * n_ref[h], axis=1, keepdims=True)
        den = jnp.sum(s, axis=1, keepdims=True) + w_inter * qn
        yield
        hc = num / jnp.maximum(jnp.abs(den), jnp.exp(-m_t))
        tot = b_c[L - 1:L, :]
        ulog_c = tot - b_c + li_c
        m_new = jnp.maximum(tot + m_prev, jnp.max(ulog_c, axis=0, keepdims=True))
        wu_c = jnp.exp(ulog_c - m_new)
        dec = jnp.exp(tot + m_prev - m_new)
        kw = k * wu_c
        yield
        c_ref[h] = dec * c_prev + _dot_tn(kw.astype(BF16), vb)
        n_ref[h] = dec * n_ref[h] + jnp.sum(kw, axis=0, keepdims=True)
        m_ref[h] = m_new
        yield
        ms = jnp.mean(hc * hc, axis=1, keepdims=True)
        hn = hc * lax.rsqrt(ms + EPS) * ng_ref[:, h * dv:(h + 1) * dv]
        og = og_ref[:, h * dv:(h + 1) * dv].astype(F32)
        o_ref[:, h * dv:(h + 1) * dv] = (_sigmoid(og) * hn).astype(o_ref.dtype)

    _run_interleaved([head_stages(h) for h in range(ML_HEADS)])


def mlstm_core(proj, gates, gate_b, norm_g, bsz, seq):
    H, L = ML_HEADS, ML_CHUNK
    t = proj.shape[0]
    dq = proj.shape[1] // (6 * H)
    dv = 2 * dq
    nc = seq // L
    gb = jnp.pad(gate_b, (0, LANES - gate_b.shape[0])).reshape(1, LANES)
    rowmap = lambda b, c: b * nc + c
    return pl.pallas_call(
        functools.partial(_mlstm_kernel, dq=dq, dv=dv),
        grid=(bsz, nc),
        in_specs=[
            pl.BlockSpec((L, H * dq), lambda b, c: (rowmap(b, c), 0)),
            pl.BlockSpec((L, H * dq), lambda b, c: (rowmap(b, c), 1)),
            pl.BlockSpec((L, H * dv), lambda b, c: (rowmap(b, c), 1)),
            pl.BlockSpec((L, H * dv), lambda b, c: (rowmap(b, c), 2)),
            pl.BlockSpec((L, LANES), lambda b, c: (rowmap(b, c), 0)),
            pl.BlockSpec((1, LANES), lambda b, c: (0, 0)),
            pl.BlockSpec((1, H * dv), lambda b, c: (0, 0)),
        ],
        out_specs=pl.BlockSpec((L, H * dv), lambda b, c: (rowmap(b, c), 0)),
        out_shape=jax.ShapeDtypeStruct((t, H * dv), BF16),
        scratch_shapes=[
            pltpu.VMEM((H, dq, dv), F32),
            pltpu.VMEM((H, 1, dq), F32),
            pltpu.VMEM((H, 1, 1), F32),
        ],
        compiler_params=_cparams(("parallel", "arbitrary")),
        name="mlstm_core",
    )(proj, proj, proj, proj, gates, gb, norm_g.reshape(1, H * dv))


def mlstm_layer(h, g_mix, w_in, gate_b, norm_g, w_out, bsz, seq):
    nmain = w_in.shape[1] - 2 * ML_HEADS
    proj = norm_matmul(h, g_mix, w_in[:, :nmain].astype(BF16), out_dtype=BF16)
    gates = norm_matmul(h, g_mix, _pad_cols(w_in[:, nmain:], LANES).astype(BF16))
    y = mlstm_core(proj, gates, gate_b, norm_g, bsz, seq)
    return matmul_residual(y, w_out.astype(BF16), h)


M2_CHUNK = 128
M2_GROUPS = 8
M2_STATE = 128
M2_HEAD_DIM = 64
CONV_K = 4
CARRY_ROWS = 8


def _causal_conv_silu(x_raw, carry_ref, w, b, cols=slice(None)):
    n = x_raw.shape[0]
    x_raw = x_raw.astype(F32)
    xe = jnp.concatenate([carry_ref[:, cols], x_raw], axis=0)
    acc = x_raw * w[CONV_K - 1:CONV_K, :]
    if b is not None:
        acc = acc + b
    for k in range(CONV_K - 1):
        sh = CONV_K - 1 - k
        acc = acc + pltpu.roll(xe, sh, axis=0)[CARRY_ROWS:, :] * w[k:k + 1, :]
    carry_ref[:, cols] = x_raw[n - CARRY_ROWS:, :]
    return _silu(acc)


def _ssd_kernel(z_ref, xs_ref, bm_ref, cm_ref, dtc_ref, dtr_ref, cwx_ref, cwb_ref, cwc_ref, cbx_ref, cbb_ref, cbc_ref,
                dtbc_ref, dtbr_ref, alogc_ref, alogr_ref, dsk_ref, ng_ref, o_ref, st_ref, cx_ref, cbm_ref, ccm_ref):
    L, P = M2_CHUNK, M2_HEAD_DIM
    PW = 2 * P
    npair = st_ref.shape[1] // PW
    assert L == PW == st_ref.shape[0]

    @pl.when(pl.program_id(2) == 0)
    def _():
        st_ref[...] = jnp.zeros_like(st_ref)
        cx_ref[...] = jnp.zeros_like(cx_ref)
        cbm_ref[...] = jnp.zeros_like(cbm_ref)
        ccm_ref[...] = jnp.zeros_like(ccm_ref)

    xs = _causal_conv_silu(xs_ref[...], cx_ref, cwx_ref[...], cbx_ref[...])
    bm = _causal_conv_silu(bm_ref[...], cbm_ref, cwb_ref[...], cbb_ref[...])
    cm = _causal_conv_silu(cm_ref[...], ccm_ref, cwc_ref[...], cbc_ref[...])
    dt_c = _softplus(dtc_ref[0, 0] + dtbc_ref[0])
    dt_r = _softplus(dtr_ref[0, 0] + dtbr_ref[0])
    lac_c = dt_c * (-jnp.exp(alogc_ref[0]))
    lac_r = dt_r * (-jnp.exp(alogr_ref[0]))
    row = _iota2((L, L), 0)
    col = _iota2((L, L), 1)
    tri = row >= col
    cum_c = _dot(tri.astype(F32), lac_c, precision=HIGHEST)
    cum_r = _dot(lac_r, (row <= col).astype(F32), precision=HIGHEST)
    cl_c = cum_c[L - 1:L, :]
    cl_r = cum_r[:, L - 1:L]
    cb = _dot_nt(cm.astype(BF16), bm.astype(BF16))
    bmt = bm.T
    st_all = st_ref[...]
    yi = _dot(cm.astype(BF16), st_all.astype(BF16))
    first = _iota2((1, PW), 1) < P

    prs = range(npair)
    colb = [[jnp.broadcast_to(cum_c[:, h:h + 1], (L, L)) for h in (2 * j, 2 * j + 1)] for j in prs]
    rowv = [[cum_r[h:h + 1, :] for h in (2 * j, 2 * j + 1)] for j in prs]
    dtv = [[dt_r[h:h + 1, :] for h in (2 * j, 2 * j + 1)] for j in prs]
    dec = [[jnp.exp(jnp.where(tri, colb[j][i] - rowv[j][i], NEG)) for i in range(2)] for j in prs]
    mm = [[cb * dec[j][i] * dtv[j][i] for i in range(2)] for j in prs]
    ww = [[bmt * (dtv[j][i] * jnp.exp(cl_r[2 * j + i:2 * j + i + 1, :] - rowv[j][i])) for i in range(2)] for j in prs]
    lhs = [jnp.concatenate([jnp.concatenate(mm[j], axis=1), jnp.concatenate(ww[j], axis=1)], axis=0).astype(BF16)
           for j in prs]
    xp = [xs[:, j * PW:(j + 1) * PW] for j in prs]
    rhs = [jnp.concatenate([jnp.where(first, xp[j], 0.0), jnp.where(first, 0.0, xp[j])], axis=0).astype(BF16)
           for j in prs]
    out = [_dot(lhs[j], rhs[j]) for j in prs]
    esc = [jnp.where(first, jnp.exp(colb[j][0]), jnp.exp(colb[j][1])) for j in prs]
    sdec = [jnp.where(first, jnp.exp(cl_c[:, 2 * j:2 * j + 1]), jnp.exp(cl_c[:, 2 * j + 1:2 * j + 2])) for j in prs]
    y = jnp.concatenate([out[j][:L] + yi[:, j * PW:(j + 1) * PW] * esc[j] for j in prs], axis=1)
    st_ref[...] = jnp.concatenate([st_all[:, j * PW:(j + 1) * PW] * sdec[j] + out[j][L:] for j in prs], axis=1)
    y = y + dsk_ref[...] * xs
    y = y * _silu(z_ref[...].astype(F32))
    ms = jnp.mean(y * y, axis=1, keepdims=True)
    o_ref[...] = (y * lax.rsqrt(ms + EPS) * ng_ref[...]).astype(o_ref.dtype)


def ssd_core(proj, dt_raw, conv_w, conv_b, dt_bias, a_log, d_skip, norm_g, bsz, seq):
    L, G, N, P = M2_CHUNK, M2_GROUPS, M2_STATE, M2_HEAD_DIM
    t = proj.shape[0]
    heads = dt_raw.shape[1]
    R = heads // G
    di = heads * P
    gw = di // G
    nc = seq // L
    dtc = dt_raw.reshape(bsz, seq, G, R).transpose(0, 2, 1, 3)
    dtr = dt_raw.reshape(bsz, seq, G, R).transpose(0, 2, 3, 1)
    cw = conv_w
    cbias = conv_b.reshape(1, -1)
    row = lambda b, g, c: b * nc + c
    nzx = di // gw
    nb0 = 2 * di // N
    cwb0 = di // N
    return pl.pallas_call(
        _ssd_kernel,
        grid=(bsz, G, nc),
        in_specs=[
            pl.BlockSpec((L, gw), lambda b, g, c: (row(b, g, c), g)),
            pl.BlockSpec((L, gw), lambda b, g, c: (row(b, g, c), nzx + g)),
            pl.BlockSpec((L, N), lambda b, g, c: (row(b, g, c), nb0 + g)),
            pl.BlockSpec((L, N), lambda b, g, c: (row(b, g, c), nb0 + G + g)),
            pl.BlockSpec((1, 1, L, R), lambda b, g, c: (b, g, c, 0)),
            pl.BlockSpec((1, 1, R, L), lambda b, g, c: (b, g, 0, c)),
            pl.BlockSpec((CONV_K, gw), lambda b, g, c: (0, g)),
            pl.BlockSpec((CONV_K, N), lambda b, g, c: (0, cwb0 + g)),
            pl.BlockSpec((CONV_K, N), lambda b, g, c: (0, cwb0 + G + g)),
            pl.BlockSpec((1, gw), lambda b, g, c: (0, g)),
            pl.BlockSpec((1, N), lambda b, g, c: (0, cwb0 + g)),
            pl.BlockSpec((1, N), lambda b, g, c: (0, cwb0 + G + g)),
            pl.BlockSpec((1, 1, R), lambda b, g, c: (g, 0, 0)),
            pl.BlockSpec((1, R, 1), lambda b, g, c: (g, 0, 0)),
            pl.BlockSpec((1, 1, R), lambda b, g, c: (g, 0, 0)),
            pl.BlockSpec((1, R, 1), lambda b, g, c: (g, 0, 0)),
            pl.BlockSpec((1, gw), lambda b, g, c: (0, g)),
            pl.BlockSpec((1, gw), lambda b, g, c: (0, g)),
        ],
        out_specs=pl.BlockSpec((L, gw), lambda b, g, c: (row(b, g, c), g)),
        out_shape=jax.ShapeDtypeStruct((t, di), BF16),
        scratch_shapes=[
            pltpu.VMEM((N, gw), F32),
            pltpu.VMEM((CARRY_ROWS, gw), F32),
            pltpu.VMEM((CARRY_ROWS, N), F32),
            pltpu.VMEM((CARRY_ROWS, N), F32),
        ],
        compiler_params=_cparams(("parallel", "parallel", "arbitrary")),
        name="ssd_core",
    )(proj, proj, proj, proj, dtc, dtr, cw, cw, cw, cbias, cbias, cbias,
      dt_bias.reshape(G, 1, R), dt_bias.reshape(G, R, 1), a_log.reshape(G, 1, R), a_log.reshape(G, R, 1),
      jnp.repeat(d_skip, P).reshape(1, di), norm_g.reshape(1, di))


def mamba2_layer(h, g_mix, w_in, conv_w, conv_b, dt_bias, a_log, d_skip, norm_g, w_out, bsz, seq):
    heads = dt_bias.shape[0]
    nmain = w_in.shape[1] - heads
    proj = norm_matmul(h, g_mix, w_in[:, :nmain].astype(BF16), out_dtype=BF16)
    dt_raw = norm_matmul(h, g_mix, _pad_cols(w_in[:, nmain:], LANES).astype(BF16))[:, :heads]
    y = ssd_core(proj, dt_raw, conv_w, conv_b, dt_bias, a_log, d_skip, norm_g, bsz, seq)
    return matmul_residual(y, w_out.astype(BF16), h)


GDN_CHUNK = 64
GDN_DH = 128
GDN_STACK = 4
GDN_STACKS = 8
GDN_STAGGER = (0, 0, 0, 0, 8, 8, 8, 8)


def _l2norm(x):
    return x * lax.rsqrt(jnp.sum(x * x, axis=1, keepdims=True) + EPS)


def _gdn_kernel(q_ref, k_ref, v_ref, z_ref, ba_ref, cwq_ref, cwk_ref, cwv_ref, alog_ref, dtb_ref, ng_ref,
                o_ref, st_ref, cq_ref, ck_ref, cv_ref):
    L, DH, HS = GDN_CHUNK, GDN_DH, GDN_STACK
    nv = GDN_STACK * GDN_STACKS
    SL = HS * L

    @pl.when(pl.program_id(2) == 0)
    def _():
        st_ref[...] = jnp.zeros_like(st_ref)
        cq_ref[...] = jnp.zeros_like(cq_ref)
        ck_ref[...] = jnp.zeros_like(ck_ref)
        cv_ref[...] = jnp.zeros_like(cv_ref)

    ba = ba_ref[0, 0]
    beta = _sigmoid(ba[:, :nv])
    gg = -jnp.exp(alog_ref[0]) * _softplus(ba[:, nv:2 * nv] + dtb_ref[0])

    ri = _iota2((SL, SL), 0)
    ci = _iota2((SL, SL), 1)
    same = (ri // L) == (ci // L)
    incl = same & (ri >= ci)
    strict = same & (ri > ci)
    eye = ri == ci
    last = same & ((ci % L) == L - 1)

    heads = range(HS)
    rows = [slice(i * L, (i + 1) * L) for i in heads]
    head_of_row = _iota2((SL, 1), 0) // L

    def stack_stages(s):
        hv0 = s * HS
        kc = slice((hv0 // 2) * DH, (hv0 // 2 + HS // 2) * DH)
        vc = slice(hv0 * DH, (hv0 + HS) * DH)
        q = _causal_conv_silu(q_ref[:, kc], cq_ref, cwq_ref[:, kc], None, kc)
        k = _causal_conv_silu(k_ref[:, kc], ck_ref, cwk_ref[:, kc], None, kc)
        yield
        v = _causal_conv_silu(v_ref[:, vc], cv_ref, cwv_ref[:, vc], None, vc)
        yield
        qn = [_l2norm(q[:, i * DH:(i + 1) * DH]) * DH ** -0.5 for i in range(HS // 2)]
        kn = [_l2norm(k[:, i * DH:(i + 1) * DH]) for i in range(HS // 2)]
        kst = jnp.concatenate([kn[i // 2] for i in heads], axis=0)
        qst = jnp.concatenate([qn[i // 2] for i in heads], axis=0)
        vst = jnp.concatenate([v[:, i * DH:(i + 1) * DH] for i in heads], axis=0)
        beta_c = jnp.concatenate([beta[:, hv0 + i:hv0 + i + 1] for i in heads], axis=0)
        g_c = jnp.concatenate([gg[:, hv0 + i:hv0 + i + 1] for i in heads], axis=0)
        yield
        g_r = _col_to_row(g_c, eye)
        yield
        cum_c = jnp.sum(jnp.where(incl, g_r, 0.0), axis=1, keepdims=True)
        yield
        cum_r = _col_to_row(cum_c, eye)
        yield
        dmask = jnp.exp(jnp.where(incl, cum_c - cum_r, NEG))
        kb = kst * beta_c
        kst_b = kst.astype(BF16)
        yield
        nmat = jnp.where(strict, _dot_nt(kb.astype(BF16), kst_b) * dmask, 0.0)
        ecum = jnp.exp(cum_c)
        rhs = jnp.concatenate([vst * beta_c, kb * ecum], axis=1)
        yield
        pw_b = nmat.astype(BF16)
        sol = rhs - _dot(pw_b, rhs.astype(BF16))
        yield
        for _ in range(5):
            pw_b = _dot(pw_b, pw_b).astype(BF16)
            sol = sol + _dot(pw_b, sol.astype(BF16))
            yield
        qe = qst * ecum
        attn = _dot_nt(qst.astype(BF16), kst_b) * dmask
        cl_c = jnp.sum(jnp.where(last, cum_r, 0.0), axis=1, keepdims=True)
        kd = kst * jnp.exp(cl_c - cum_c)
        st_all = st_ref[s]
        yield
        both = [_dot(jnp.concatenate([sol[rows[i], DH:], qe[rows[i]]], axis=0).astype(BF16),
                     st_all[:, i * DH:(i + 1) * DH].astype(BF16)) for i in heads]
        yield
        vnew = jnp.concatenate([sol[rows[i], :DH] - both[i][:L] for i in heads], axis=0)
        o = jnp.concatenate([both[i][L:] for i in heads], axis=0) + _dot(attn.astype(BF16), vnew.astype(BF16))
        yield
        vbd = jnp.concatenate([jnp.where(head_of_row == i, vnew, 0.0) for i in heads], axis=1)
        decay = jnp.concatenate(
            [jnp.broadcast_to(jnp.exp(cl_c[i * L:i * L + 1, :]), (1, DH)) for i in heads], axis=1)
        st_ref[s] = st_all * decay + _dot(kd.T.astype(BF16), vbd.astype(BF16))
        yield
        for i in heads:
            hv = hv0 + i
            oi = o[rows[i]]
            ms = jnp.mean(oi * oi, axis=1, keepdims=True)
            on = oi * lax.rsqrt(ms + EPS) * ng_ref[...]
            zi = z_ref[:, hv * DH:(hv + 1) * DH].astype(F32)
            o_ref[:, hv * DH:(hv + 1) * DH] = (on * _silu(zi)).astype(o_ref.dtype)

    _run_interleaved([stack_stages(s) for s in range(GDN_STACKS)], GDN_STAGGER)


def gdn_core(proj, ba_raw, conv_w, a_log, dt_bias, norm_g, bsz, seq):
    L, DH = GDN_CHUNK, GDN_DH
    nv = GDN_STACK * GDN_STACKS
    t = proj.shape[0]
    hv_total = ba_raw.shape[1] // 2
    ng = hv_total // nv
    qw = (nv // 2) * DH
    vw = nv * DH
    nc = seq // L
    b_part = ba_raw[:, :hv_total].reshape(bsz, seq, ng, nv)
    a_part = ba_raw[:, hv_total:].reshape(bsz, seq, ng, nv)
    ba = jnp.concatenate([b_part, a_part], axis=-1).transpose(0, 2, 1, 3)
    row = lambda b, g, c: b * nc + c
    kq0 = ng
    v0 = 2 * ng * qw // vw
    return pl.pallas_call(
        _gdn_kernel,
        grid=(bsz, ng, nc),
        in_specs=[
            pl.BlockSpec((L, qw), lambda b, g, c: (row(b, g, c), g)),
            pl.BlockSpec((L, qw), lambda b, g, c: (row(b, g, c), kq0 + g)),
            pl.BlockSpec((L, vw), lambda b, g, c: (row(b, g, c), v0 + g)),
            pl.BlockSpec((L, vw), lambda b, g, c: (row(b, g, c), v0 + ng + g)),
            pl.BlockSpec((1, 1, L, 2 * nv), lambda b, g, c: (b, g, c, 0)),
            pl.BlockSpec((CONV_K, qw), lambda b, g, c: (0, g)),
            pl.BlockSpec((CONV_K, qw), lambda b, g, c: (0, kq0 + g)),
            pl.BlockSpec((CONV_K, vw), lambda b, g, c: (0, v0 + g)),
            pl.BlockSpec((1, 1, nv), lambda b, g, c: (g, 0, 0)),
            pl.BlockSpec((1, 1, nv), lambda b, g, c: (g, 0, 0)),
            pl.BlockSpec((1, DH), lambda b, g, c: (0, 0)),
        ],
        out_specs=pl.BlockSpec((L, vw), lambda b, g, c: (row(b, g, c), g)),
        out_shape=jax.ShapeDtypeStruct((t, hv_total * DH), BF16),
        scratch_shapes=[
            pltpu.VMEM((GDN_STACKS, DH, GDN_STACK * DH), F32),
            pltpu.VMEM((CARRY_ROWS, qw), F32),
            pltpu.VMEM((CARRY_ROWS, qw), F32),
            pltpu.VMEM((CARRY_ROWS, vw), F32),
        ],
        compiler_params=_cparams(("parallel", "parallel", "arbitrary")),
        name="gdn_core",
    )(proj, proj, proj, proj, ba, conv_w, conv_w, conv_w,
      a_log.reshape(ng, 1, nv), dt_bias.reshape(ng, 1, nv), norm_g.reshape(1, DH))


def gdn_layer(h, g_mix, w_in, conv_w, a_log, dt_bias, norm_g, w_out, bsz, seq):
    hv = a_log.shape[0]
    nmain = w_in.shape[1] - 2 * hv
    proj = norm_matmul(h, g_mix, w_in[:, :nmain].astype(BF16), out_dtype=BF16)
    ba_raw = norm_matmul(h, g_mix, _pad_cols(w_in[:, nmain:], LANES).astype(BF16))[:, :2 * hv]
    y = gdn_core(proj, ba_raw, conv_w, a_log, dt_bias, norm_g, bsz, seq)
    return matmul_residual(y, w_out.astype(BF16), h)


NSA_DK = 128
NSA_G = 4
NSA_HPG = 4
NSA_STRIDE = 16
NSA_CMP_BLOCK = 32
NSA_SEL = 64
NSA_TOP_N = 16
NSA_WINDOW = 512
NSA_TQ = 128
NSA_TKS = 512
NSA_TKW = 128
FORCE_BONUS = 1e4


def _cmp_kernel(x_ref, pa_ref, pb_ref, wa_ref, wb_ref, w2_ref, kg_ref, o_ref, acca_ref, accb_ref):
    kk = pl.program_id(2)

    @pl.when(kk == 0)
    def _():
        acca_ref[...] = jnp.zeros_like(acca_ref)
        accb_ref[...] = jnp.zeros_like(accb_ref)

    x = x_ref[0, 0].astype(F32)
    acca_ref[...] += _dot((x + pa_ref[0]).astype(BF16), wa_ref[0])
    accb_ref[...] += _dot((x + pb_ref[0]).astype(BF16), wb_ref[0])

    @pl.when(kk == pl.num_programs(2) - 1)
    def _():
        nrow = acca_ref.shape[0]
        pre = acca_ref[...] + pltpu.roll(accb_ref[...], nrow - 1, axis=0)
        y = _dot(_silu(pre).astype(BF16), w2_ref[0])
        is_k = pl.program_id(0) == 0
        for g in range(NSA_G):
            yg = y[:, g * NSA_DK:(g + 1) * NSA_DK]
            ms = jnp.mean(yg * yg, axis=1, keepdims=True)
            yn = yg * lax.rsqrt(ms + EPS) * kg_ref[...]
            o_ref[0, 0, :, g * NSA_DK:(g + 1) * NSA_DK] = jnp.where(is_k, yn, yg)


def nsa_compress(xkv, pos, w1, w2, kg0, *, tk=2048):
    _, bsz, ngrp, kdim = xkv.shape
    G, dk, half = NSA_G, NSA_DK, NSA_CMP_BLOCK // 2
    eye = jnp.eye(G, dtype=w1.dtype)

    def expand_w1(w):
        return jnp.einsum("clde,gh->clgdhe", w, eye).reshape(2, half * G * dk, G * dk).astype(BF16)

    def expand_pos(ps):
        return jnp.broadcast_to(ps[:, :, None, :], (2, half, G, dk)).reshape(2, 1, half * G * dk)

    wa, wb = expand_w1(w1[:, :half]), expand_w1(w1[:, half:])
    pa, pb = expand_pos(pos[:, :half]), expand_pos(pos[:, half:])
    w2bd = jnp.einsum("cde,gh->cgdhe", w2, eye).reshape(2, G * dk, G * dk).astype(BF16)
    tk = min(tk, kdim)
    assert kdim % tk == 0
    return pl.pallas_call(
        _cmp_kernel,
        grid=(2, bsz, kdim // tk),
        in_specs=[
            pl.BlockSpec((1, 1, ngrp, tk), lambda c, b, l: (c, b, 0, l)),
            pl.BlockSpec((1, 1, tk), lambda c, b, l: (c, 0, l)),
            pl.BlockSpec((1, 1, tk), lambda c, b, l: (c, 0, l)),
            pl.BlockSpec((1, tk, G * dk), lambda c, b, l: (c, l, 0)),
            pl.BlockSpec((1, tk, G * dk), lambda c, b, l: (c, l, 0)),
            pl.BlockSpec((1, G * dk, G * dk), lambda c, b, l: (c, 0, 0)),
            pl.BlockSpec((1, dk), lambda c, b, l: (0, 0)),
        ],
        out_specs=pl.BlockSpec((1, 1, ngrp, G * dk), lambda c, b, l: (c, b, 0, 0)),
        out_shape=jax.ShapeDtypeStruct((2, bsz, ngrp, G * dk), F32),
        scratch_shapes=[pltpu.VMEM((ngrp, G * dk), F32), pltpu.VMEM((ngrp, G * dk), F32)],
        compiler_params=_cparams(("parallel", "parallel", "arbitrary")),
        name="nsa_compress",
    )(xkv, pa, pb, wa, wb, w2bd, kg0.reshape(1, dk))


def _nsa_attn_kernel(q_ref, ks_ref, vs_ref, kw_ref, vw_ref, kc_ref, vc_ref, gate_ref, qg_ref, kg_ref, o_ref,
                     ksn_ref, vst_ref, kwn_ref, vwt_ref, kcb_ref, vct_ref, sel_ref, *, n_top):
    TQ, TS, TW, DK, HPG = NSA_TQ, NSA_TKS, NSA_TKW, NSA_DK, NSA_HPG
    i = pl.program_id(2)
    ncmp = kcb_ref.shape[0]
    nsel = sel_ref.shape[0]
    rows = HPG * TQ

    @pl.when(i == 0)
    def _():
        def norm_key(kt, gidx):
            ms = jnp.mean(kt * kt, axis=1, keepdims=True)
            return (kt * lax.rsqrt(ms + EPS) * kg_ref[gidx:gidx + 1, :]).astype(BF16)

        def prep_sel(t, carry):
            sl = pl.ds(pl.multiple_of(t * TS, TS), TS)
            ksn_ref[t] = norm_key(ks_ref[sl, :].astype(F32), 1)
            vst_ref[t] = vs_ref[sl, :].astype(F32).T.astype(BF16)
            return carry

        def prep_win(t, carry):
            sl = pl.ds(pl.multiple_of(t * TW, TW), TW)
            kwn_ref[t] = norm_key(kw_ref[sl, :].astype(F32), 2)
            vwt_ref[t] = vw_ref[sl, :].astype(F32).T.astype(BF16)
            return carry

        lax.fori_loop(0, ksn_ref.shape[0], prep_sel, 0)
        lax.fori_loop(0, kwn_ref.shape[0], prep_win, 0)
        kcb_ref[...] = kc_ref[0, 0].astype(BF16)
        vct_ref[...] = vc_ref[0, 0].T.astype(BF16)

    s0 = i * TQ
    q = q_ref[...].astype(F32)
    scale = DK ** -0.5
    qparts = []
    for r in range(HPG):
        qr = q[:, r * DK:(r + 1) * DK]
        ms = jnp.mean(qr * qr, axis=1, keepdims=True)
        qparts.append(qr * lax.rsqrt(ms + EPS) * (qg_ref[...] * scale))
    qrows = jnp.concatenate(qparts, axis=0).astype(BF16)
    tq = s0 + (_iota2((1, rows), 1) % TQ)

    res = {}

    def compressed_stages():
        sc = _dot_nt(kcb_ref[...], qrows)
        c_end = _iota2((ncmp, 1), 0) * NSA_STRIDE + (NSA_CMP_BLOCK - 1)
        maskc = c_end <= tq
        yield
        sc = jnp.where(maskc, sc, NEG)
        e = jnp.where(maskc, jnp.exp(sc - jnp.max(sc, axis=0, keepdims=True)), 0.0)
        yield
        pc = e / jnp.maximum(jnp.sum(e, axis=0, keepdims=True), 1e-30)
        res["oc"] = _dot(vct_ref[...], pc.astype(BF16))
        yield
        pcs = pc[:, 0:TQ]
        for r in range(1, HPG):
            pcs = pcs + pc[:, r * TQ:(r + 1) * TQ]
        jj = _iota2((nsel, ncmp), 0) * NSA_SEL
        cs = _iota2((nsel, ncmp), 1) * NSA_STRIDE
        overlap = ((cs < jj + NSA_SEL) & (cs + (NSA_CMP_BLOCK - 1) >= jj)).astype(F32)
        imp = _dot(overlap, pcs, precision=HIGHEST)
        yield
        jcol = _iota2((nsel, TQ), 0)
        cur = (s0 + _iota2((nsel, TQ), 1)) // NSA_SEL
        forced = (jcol == 0) | (jcol == cur) | (jcol == cur - 1)
        imp = jnp.where(jcol <= cur, imp + jnp.where(forced, FORCE_BONUS, 0.0), NEG)
        rank = jnp.zeros((nsel, TQ), F32)
        for j in range(nsel):
            vj = imp[j:j + 1, :]
            beats = (vj > imp) | ((vj == imp) & (jcol > j))
            rank = rank + beats.astype(F32)
            if j % 8 == 7:
                yield
        sel_ref[...] = jnp.where(rank < n_top, 0.0, NEG)

    def window_stages():
        nwt = (NSA_WINDOW + TQ) // TW
        kt0 = i - NSA_WINDOW // TW
        idx = [jnp.maximum(kt0 + d, 0) for d in range(nwt)]
        sw = _dot_nt(jnp.concatenate([kwn_ref[ix] for ix in idx], axis=0), qrows)
        yield
        kofs = _iota2((TW, 1), 0)
        parts = []
        for d in range(nwt):
            kt = kt0 + d
            sd = sw[d * TW:(d + 1) * TW]
            if d == 0:
                sd = jnp.where((kt * TW + kofs > tq - NSA_WINDOW) & (kt >= 0), sd, NEG)
            elif d == nwt - 1:
                sd = jnp.where(kt * TW + kofs <= tq, sd, NEG)
            else:
                sd = sd + jnp.where(kt >= 0, 0.0, NEG)
            parts.append(sd)
        sw = jnp.concatenate(parts, axis=0)
        yield
        pw = jnp.exp(sw - jnp.max(sw, axis=0, keepdims=True))
        yield
        res["l_w"] = jnp.sum(pw, axis=0, keepdims=True)
        pwb = pw.astype(BF16)
        acc_w = _dot(vwt_ref[idx[0]], pwb[0:TW])
        for d in range(1, nwt):
            acc_w = acc_w + _dot(vwt_ref[idx[d]], pwb[d * TW:(d + 1) * TW])
        res["acc_w"] = acc_w

    _run_interleaved([compressed_stages(), window_stages()])
    oc, l_w, acc_w = res["oc"], res["l_w"], res["acc_w"]

    bpt = TS // NSA_SEL

    def sel_scores(kt):
        s = _dot_nt(ksn_ref[kt], qrows)
        parts = []
        for jb in range(bpt):
            srow = sel_ref[pl.ds(kt * bpt + jb, 1), :]
            srow = jnp.concatenate([srow] * HPG, axis=1)
            parts.append(jnp.broadcast_to(srow, (NSA_SEL, rows)))
        return s + jnp.concatenate(parts, axis=0)

    def online(s, carry, vt):
        m, l, acc = carry
        m_new = jnp.maximum(m, jnp.max(s, axis=0, keepdims=True))
        alpha = jnp.exp(m - m_new)
        pexp = jnp.exp(s - m_new)
        l = alpha * l + jnp.sum(pexp, axis=0, keepdims=True)
        acc = alpha * acc + _dot(vt, pexp.astype(BF16))
        return m_new, l, acc

    def sel_body(kt, carry):
        return online(sel_scores(kt), carry, vst_ref[kt])

    init = (jnp.full((1, rows), NEG, F32), jnp.zeros((1, rows), F32), jnp.zeros((DK, rows), F32))
    kd = s0 // TS
    carry = lax.fori_loop(0, kd, sel_body, init)
    kpos = kd * TS + _iota2((TS, 1), 0)
    _, l_s, acc_s = online(jnp.where(kpos <= tq, sel_scores(kd), NEG), carry, vst_ref[kd])

    gt = _sigmoid(gate_ref[...])

    def grow(c):
        return jnp.concatenate([gt[c * HPG + r:c * HPG + r + 1, :] for r in range(HPG)], axis=1)

    out = (grow(0) * oc + grow(1) * (acc_s / jnp.maximum(l_s, 1e-30))
           + grow(2) * (acc_w / jnp.maximum(l_w, 1e-30)))
    for r in range(HPG):
        o_ref[:, r * DK:(r + 1) * DK] = out[:, r * TQ:(r + 1) * TQ].T.astype(o_ref.dtype)


def nsa_attention(proj, kvc, gate, q_g, k_g, bsz, seq):
    G, HPG, DK, TQ, TS, TW = NSA_G, NSA_HPG, NSA_DK, NSA_TQ, NSA_TKS, NSA_TKW
    assert TQ == TW and seq % TS == 0
    t = proj.shape[0]
    nq = seq // TQ
    ncmp = kvc.shape[2]
    nsel = seq // NSA_SEL
    n_top = min(NSA_TOP_N, nsel)
    qw = HPG * DK
    kv0 = G * HPG

    def kvspec(which):
        return pl.BlockSpec((seq, DK), lambda b, g, i: (b, kv0 + which * G + g))

    return pl.pallas_call(
        functools.partial(_nsa_attn_kernel, n_top=n_top),
        grid=(bsz, G, nq),
        in_specs=[
            pl.BlockSpec((TQ, qw), lambda b, g, i: (b * nq + i, g)),
            kvspec(2), kvspec(3), kvspec(4), kvspec(5),
            pl.BlockSpec((1, 1, ncmp, DK), lambda b, g, i: (0, b, 0, g)),
            pl.BlockSpec((1, 1, ncmp, DK), lambda b, g, i: (1, b, 0, g)),
            pl.BlockSpec((4 * HPG, TQ), lambda b, g, i: (g, b * nq + i)),
            pl.BlockSpec((1, DK), lambda b, g, i: (0, 0)),
            pl.BlockSpec((3, DK), lambda b, g, i: (0, 0)),
        ],
        out_specs=pl.BlockSpec((TQ, qw), lambda b, g, i: (b * nq + i, g)),
        out_shape=jax.ShapeDtypeStruct((t, G * qw), BF16),
        scratch_shapes=[
            pltpu.VMEM((seq // TS, TS, DK), BF16),
            pltpu.VMEM((seq // TS, DK, TS), BF16),
            pltpu.VMEM((seq // TW, TW, DK), BF16),
            pltpu.VMEM((seq // TW, DK, TW), BF16),
            pltpu.VMEM((ncmp, DK), BF16),
            pltpu.VMEM((DK, ncmp), BF16),
            pltpu.VMEM((nsel, TQ), F32),
        ],
        compiler_params=_cparams(("parallel", "parallel", "arbitrary")),
        name="nsa_attention",
    )(proj, proj, proj, proj, proj, kvc, kvc, gate, q_g.reshape(1, DK), k_g)


def nsa_layer(h, g_mix, w_in, q_g, k_g, cmp_pos, cmp_w1, cmp_w2, w_out, bsz, seq):
    G, HPG, DK = NSA_G, NSA_HPG, NSA_DK
    H = G * HPG
    nmain = H * DK + 6 * G * DK
    proj = norm_matmul(h, g_mix, w_in[:, :nmain].astype(BF16), out_dtype=BF16)
    rows = range(G * 4 * HPG)
    perm = jnp.array([((n // (4 * HPG)) * HPG + n % HPG) * 3 + (n % (4 * HPG)) // HPG
                      if (n % (4 * HPG)) // HPG < 3 else LANES - 1 for n in rows] + [LANES - 1] * (LANES - len(rows)))
    gate_t = norm_matmul_t(h, g_mix, _pad_cols(w_in[:, nmain:], LANES)[:, perm].astype(BF16))
    ngrp = seq // NSA_STRIDE
    kvw = G * DK
    xkv = jnp.stack([
        proj[:, H * DK:H * DK + kvw].reshape(bsz, ngrp, NSA_STRIDE * kvw),
        proj[:, H * DK + kvw:H * DK + 2 * kvw].reshape(bsz, ngrp, NSA_STRIDE * kvw),
    ])
    kvc = nsa_compress(xkv, cmp_pos, cmp_w1, cmp_w2, k_g[0])
    o = nsa_attention(proj, kvc, gate_t, q_g, k_g, bsz, seq)
    return matmul_residual(o, w_out.astype(BF16), h)


def kernel(x, p, norm_mix_g, norm_mlp_g, mlp_w1, mlp_w2, ple_norm_g, ple_w_gate, ple_w_proj, nsa_w_in, nsa_q_g, nsa_k_g, nsa_cmp_pos, nsa_cmp_w1, nsa_cmp_w2, nsa_w_out, m2_w_in, m2_conv_w, m2_conv_b, m2_dt_bias, m2_a_log, m2_d, m2_norm_g, m2_w_out, gdn_w_in, gdn_conv_w, gdn_a_log, gdn_dt_bias, gdn_norm_g, gdn_w_out, ml_w_in, ml_gate_b, ml_norm_g, ml_w_out):
    bsz, seq, d = x.shape
    h = x.reshape(bsz * seq, d)
    for i in range(p.shape[0]):
        kind, j = i % 4, i // 4
        if kind == 0:
            h = nsa_layer(h, norm_mix_g[i], nsa_w_in[j], nsa_q_g[j], nsa_k_g[j], nsa_cmp_pos[j], nsa_cmp_w1[j], nsa_cmp_w2[j], nsa_w_out[j], bsz, seq)
        elif kind == 1:
            h = mamba2_layer(h, norm_mix_g[i], m2_w_in[j], m2_conv_w[j], m2_conv_b[j], m2_dt_bias[j], m2_a_log[j], m2_d[j], m2_norm_g[j], m2_w_out[j], bsz, seq)
        elif kind == 2:
            h = gdn_layer(h, norm_mix_g[i], gdn_w_in[j], gdn_conv_w[j], gdn_a_log[j], gdn_dt_bias[j], gdn_norm_g[j], gdn_w_out[j], bsz, seq)
        elif kind == 3:
            h = mlstm_layer(h, norm_mix_g[i], ml_w_in[j], ml_gate_b[j], ml_norm_g[j], ml_w_out[j], bsz, seq)
        h = mlp_block(h, norm_mlp_g[i], mlp_w1[i].astype(BF16), mlp_w2[i].astype(BF16))
        h = ple_update(h, ple_norm_g[i], ple_w_gate[i].astype(BF16), p[i].reshape(bsz * seq, -1), ple_w_proj[i].astype(BF16))
    return h.reshape(bsz, seq, d)
```
